```python
import math
import jax, jax.numpy as jnp
from jax import lax
import numpy as np

D_MODEL = 2048
BATCH = 2
SEQ = 4096
DEPTH = 4

F32 = jnp.float32
RMS_EPS = 1e-6
CONV_K = 4

SSD_HEADS = 32
SSD_HEAD_DIM = 64
SSD_INNER = SSD_HEADS * SSD_HEAD_DIM
SSD_GROUPS = 4
SSD_HPG = SSD_HEADS // SSD_GROUPS
SSD_STATE = 128
SSD_CHUNK = 128
SSD_CONV_DIM = SSD_INNER + 2 * SSD_GROUPS * SSD_STATE

SB_HEADS = 16
SB_HEAD_DIM = 128
SB_INNER = SB_HEADS * SB_HEAD_DIM
SB_BLOCK = 128

EV_O1 = SSD_INNER
EV_O2 = EV_O1 + SSD_CONV_DIM
EV_O3 = EV_O2 + SSD_HEADS
EV_O4 = EV_O3 + SB_INNER
EV_O5 = EV_O4 + SB_INNER
EV_IN = EV_O5 + SB_INNER
EV_SPLITS = (EV_O1, EV_O2, EV_O3, EV_O4, EV_O5)

GDN_K_HEADS = 16
GDN_V_HEADS = 32
GDN_K_DIM = 128
GDN_V_DIM = 128
GDN_KEY = GDN_K_HEADS * GDN_K_DIM
GDN_VAL = GDN_V_HEADS * GDN_V_DIM
GDN_CONV_DIM = 2 * GDN_KEY + GDN_VAL
GDN_CHUNK = 64
OD_IN = GDN_CONV_DIM + GDN_VAL + 2 * GDN_V_HEADS
OD_SPLITS = (GDN_CONV_DIM, GDN_CONV_DIM + GDN_VAL, GDN_CONV_DIM + GDN_VAL + GDN_V_HEADS)

N_EXPERTS = 32
TOP_K = 4
EXPERT_FF = D_MODEL // 2
SWIGLU_LIMIT = 7.0
SWIGLU_ALPHA = 1.702

N_EVEN = (DEPTH + 1) // 2
N_ODD = DEPTH // 2
DT_MIN = 1e-3
DT_MAX = 1e-1
A_MIN = 1.0
A_MAX = 16.0

kernel_name = 'hybrid_ssd_stickbreak_gdn_moe'


def rms_norm(x, w):
    xf = x.astype(F32)
    return xf * lax.rsqrt(jnp.mean(xf * xf, axis=-1, keepdims=True) + RMS_EPS) * w.astype(F32)


def l2_norm(x):
    return x * lax.rsqrt(jnp.sum(x * x, axis=-1, keepdims=True) + RMS_EPS)


def causal_dwconv(x, w):
    c = x.shape[-1]
    return lax.conv_general_dilated(
        x, w.astype(x.dtype)[:, None, :], window_strides=(1,),
        padding=((CONV_K - 1, 0),), dimension_numbers=('NWC', 'WIO', 'NWC'),
        feature_group_count=c)


def ssd_chunked_scan(xbc, dt_raw, dt_bias, a_log, d_skip):
    bsz, seq, _ = xbc.shape
    nc = seq // SSD_CHUNK
    xbc = xbc.astype(F32)
    xs, bm, cm = jnp.split(xbc, (SSD_INNER, SSD_INNER + SSD_GROUPS * SSD_STATE), axis=-1)
    xs = xs.reshape(bsz, nc, SSD_CHUNK, SSD_GROUPS, SSD_HPG, SSD_HEAD_DIM)
    bm = bm.reshape(bsz, nc, SSD_CHUNK, SSD_GROUPS, SSD_STATE)
    cm = cm.reshape(bsz, nc, SSD_CHUNK, SSD_GROUPS, SSD_STATE)
    dt = jax.nn.softplus(dt_raw.astype(F32) + dt_bias.astype(F32))
    dt = dt.reshape(bsz, nc, SSD_CHUNK, SSD_GROUPS, SSD_HPG)
    log_a = -jnp.exp(a_log.astype(F32)).reshape(SSD_GROUPS, SSD_HPG) * dt
    a_cs = jnp.cumsum(jnp.transpose(log_a, (0, 1, 3, 4, 2)), axis=-1)
    xdt = xs * dt[..., None]
    causal = jnp.tril(jnp.ones((SSD_CHUNK, SSD_CHUNK), bool))
    decay = jnp.exp(jnp.where(causal, a_cs[..., :, None] - a_cs[..., None, :], -jnp.inf))
    scores = jnp.einsum('bclgn,bcsgn->bcgls', cm, bm)[:, :, :, None] * decay
    y_diag = jnp.einsum('bcgrls,bcsgrp->bclgrp', scores, xdt)
    to_end = jnp.moveaxis(jnp.exp(a_cs[..., -1:] - a_cs), -1, 2)[..., None]
    states = jnp.einsum('bcsgn,bcsgrp->bcgrpn', bm, xdt * to_end)
    chunk_decay = jnp.exp(a_cs[..., -1])

    def carry_state(h, inp):
        st, dec = inp
        return h * dec[..., None, None] + st, h

    h0 = jnp.zeros_like(states[:, 0])
    _, h_in = lax.scan(carry_state, h0, (jnp.moveaxis(states, 1, 0), jnp.moveaxis(chunk_decay, 1, 0)))
    h_in = jnp.moveaxis(h_in, 0, 1)
    from_start = jnp.moveaxis(jnp.exp(a_cs), -1, 2)[..., None]
    y_off = jnp.einsum('bclgn,bcgrpn->bclgrp', cm, h_in) * from_start
    y = y_diag + y_off + xs * d_skip.astype(F32).reshape(SSD_GROUPS, SSD_HPG, 1)
    return y.reshape(bsz, seq, SSD_INNER)


def stick_breaking_attention(q, k, v):
    seq = q.shape[1]
    scale = SB_HEAD_DIM ** -0.5
    local = jnp.arange(SB_BLOCK)
    outs = []
    for start in range(0, seq, SB_BLOCK):
        stop = start + SB_BLOCK
        logits = jnp.einsum('bqhd,bkhd->bhqk', q[:, start:stop], k[:, :stop]) * scale
        strict = jnp.arange(stop)[None, :] < (start + local)[:, None]
        log_keep = jnp.where(strict, jax.nn.log_sigmoid(-logits), 0.0)
        between = lax.cumsum(log_keep, axis=3, reverse=True) - log_keep
        weights = jnp.where(strict, jnp.exp(jax.nn.log_sigmoid(logits) + between), 0.0)
        outs.append(jnp.einsum('bhqk,bkhd->bqhd', weights, v[:, :stop]))
    return jnp.concatenate(outs, axis=1)


def ssd_stickbreak_mixer(h, w_in, conv_w, conv_b, dt_bias, a_log, d_skip, ssd_norm, q_norm, k_norm, w_out):
    bsz, seq, _ = h.shape
    proj = h @ w_in
    z, xbc, dt_raw, q, k, v = jnp.split(proj, EV_SPLITS, axis=-1)
    xbc = jax.nn.silu(causal_dwconv(xbc, conv_w) + conv_b.astype(xbc.dtype))
    y_ssd = ssd_chunked_scan(xbc, dt_raw, dt_bias, a_log, d_skip)
    y_ssd = rms_norm(y_ssd * jax.nn.silu(z.astype(F32)), ssd_norm)
    qh = rms_norm(q.reshape(bsz, seq, SB_HEADS, SB_HEAD_DIM), q_norm)
    kh = rms_norm(k.reshape(bsz, seq, SB_HEADS, SB_HEAD_DIM), k_norm)
    vh = v.reshape(bsz, seq, SB_HEADS, SB_HEAD_DIM).astype(F32)
    y_sb = stick_breaking_attention(qh, kh, vh).reshape(bsz, seq, SB_INNER)
    y = jnp.concatenate([y_ssd, y_sb], axis=-1).astype(h.dtype)
    return y @ w_out


def gated_delta_rule(q, k, v, g, beta):
    bsz, seq, nh, dk = q.shape
    dv = v.shape[-1]
    nc = seq // GDN_CHUNK

    def chunks(t):
        return t.reshape(bsz, nc, GDN_CHUNK, nh, -1).transpose(0, 3, 1, 2, 4)

    qc = chunks(q * dk ** -0.5)
    kc = chunks(k)
    vc = chunks(v)
    bc = chunks(beta[..., None])
    g_cs = jnp.cumsum(chunks(g[..., None])[..., 0], axis=-1)
    incl = jnp.tril(jnp.ones((GDN_CHUNK, GDN_CHUNK), bool))
    strict = jnp.tril(jnp.ones((GDN_CHUNK, GDN_CHUNK), bool), -1)
    decay = jnp.exp(jnp.where(incl, g_cs[..., :, None] - g_cs[..., None, :], -jnp.inf))
    kb = kc * bc
    a = jnp.where(strict, jnp.einsum('bhncd,bhnsd->bhncs', kb, kc) * decay, 0.0)
    eye = jnp.eye(GDN_CHUNK, dtype=F32)
    t = lax.linalg.triangular_solve(a + eye, jnp.broadcast_to(eye, a.shape), left_side=True,
                                    lower=True, unit_diagonal=True)
    u = t @ (vc * bc)
    w = t @ (kb * jnp.exp(g_cs)[..., None])
    qk = jnp.einsum('bhncd,bhnsd->bhncs', qc, kc) * decay
    q_decayed = qc * jnp.exp(g_cs)[..., None]
    k_to_end = kc * jnp.exp(g_cs[..., -1:] - g_cs)[..., None]
    chunk_decay = jnp.exp(g_cs[..., -1])

    def step(s, inp):
        q_i, k_i, u_i, w_i, qk_i, dec_i = inp
        v_new = u_i - w_i @ s
        o_i = q_i @ s + qk_i @ v_new
        s = s * dec_i[..., None, None] + jnp.swapaxes(k_i, -1, -2) @ v_new
        return s, o_i

    s0 = jnp.zeros((bsz, nh, dk, dv), F32)
    xs = (jnp.moveaxis(q_decayed, 2, 0), jnp.moveaxis(k_to_end, 2, 0), jnp.moveaxis(u, 2, 0),
          jnp.moveaxis(w, 2, 0), jnp.moveaxis(qk, 2, 0), jnp.moveaxis(chunk_decay, 2, 0))
    _, o = lax.scan(step, s0, xs)
    return o.transpose(1, 0, 3, 2, 4).reshape(bsz, seq, nh, dv)


def gated_deltanet_mixer(h, w_in, conv_w, dt_bias, a_log, out_norm, w_out):
    bsz, seq, _ = h.shape
    proj = h @ w_in
    qkv, z, b_raw, a_raw = jnp.split(proj, OD_SPLITS, axis=-1)
    qkv = jax.nn.silu(causal_dwconv(qkv, conv_w)).astype(F32)
    q, k, v = jnp.split(qkv, (GDN_KEY, 2 * GDN_KEY), axis=-1)
    rep = GDN_V_HEADS // GDN_K_HEADS
    q = jnp.repeat(l2_norm(q.reshape(bsz, seq, GDN_K_HEADS, GDN_K_DIM)), rep, axis=2)
    k = jnp.repeat(l2_norm(k.reshape(bsz, seq, GDN_K_HEADS, GDN_K_DIM)), rep, axis=2)
    v = v.reshape(bsz, seq, GDN_V_HEADS, GDN_V_DIM)
    beta = jax.nn.sigmoid(b_raw.astype(F32))
    g = -jnp.exp(a_log.astype(F32)) * jax.nn.softplus(a_raw.astype(F32) + dt_bias.astype(F32))
    o = gated_delta_rule(q, k, v, g, beta)
    o = rms_norm(o, out_norm) * jax.nn.silu(z.astype(F32).reshape(bsz, seq, GDN_V_HEADS, GDN_V_DIM))
    return o.reshape(bsz, seq, GDN_VAL).astype(h.dtype) @ w_out


def moe_ffn(h, w_router, b_router, w_gate_up, b_gate_up, w_down, b_down):
    bsz, seq, d = h.shape
    t = h.reshape(bsz * seq, d)
    logits = (t @ w_router).astype(F32) + b_router.astype(F32)
    top_val, top_idx = lax.top_k(logits, TOP_K)
    gates = jax.nn.softmax(top_val, axis=-1)
    combine = jnp.sum(jax.nn.one_hot(top_idx, N_EXPERTS, dtype=F32) * gates[..., None], axis=1)
    out = jnp.zeros((bsz * seq, d), F32)
    for e in range(N_EXPERTS):
        gu = t @ w_gate_up[e] + b_gate_up[e]
        gate = jnp.minimum(gu[:, :EXPERT_FF], SWIGLU_LIMIT)
        up = jnp.clip(gu[:, EXPERT_FF:], -SWIGLU_LIMIT, SWIGLU_LIMIT)
        act = (up + 1.0) * gate * jax.nn.sigmoid(SWIGLU_ALPHA * gate)
        out = out + combine[:, e:e + 1] * (act @ w_down[e] + b_down[e])
    return out.reshape(bsz, seq, d).astype(h.dtype)


def setup_inputs(seed: int = 0) -> dict:
    key = jax.random.key(seed)
    keys = iter(jax.random.split(key, 32))

    def uni(shape, std):
        bound = std * math.sqrt(3.0)
        return jax.random.uniform(next(keys), shape, jnp.float32, -bound, bound)

    def gain(shape, noise=0.02):
        return 1.0 + noise * jax.random.normal(next(keys), shape, jnp.float32)

    def small(shape):
        return 0.01 * jax.random.normal(next(keys), shape, jnp.float32)

    def dt_bias(shape):
        dt = jnp.exp(jax.random.uniform(next(keys), shape, jnp.float32, math.log(DT_MIN), math.log(DT_MAX)))
        return dt + jnp.log(-jnp.expm1(-dt))

    def a_log(shape):
        return jnp.log(jax.random.uniform(next(keys), shape, jnp.float32, A_MIN, A_MAX))

    out_std = (2 * DEPTH) ** -0.5
    return {
        'x': jax.random.normal(next(keys), (BATCH, SEQ, D_MODEL), jnp.float32),
        'mix_norm': gain((DEPTH, D_MODEL)),
        'ffn_norm': gain((DEPTH, D_MODEL)),
        'ev_w_in': uni((N_EVEN, D_MODEL, EV_IN), D_MODEL ** -0.5),
        'ev_conv_w': uni((N_EVEN, CONV_K, SSD_CONV_DIM), CONV_K ** -0.5),
        'ev_conv_b': small((N_EVEN, SSD_CONV_DIM)),
        'ev_dt_bias': dt_bias((N_EVEN, SSD_HEADS)),
        'ev_a_log': a_log((N_EVEN, SSD_HEADS)),
        'ev_d_skip': gain((N_EVEN, SSD_HEADS), 0.1),
        'ev_ssd_norm': gain((N_EVEN, SSD_INNER)),
        'ev_q_norm': gain((N_EVEN, SB_HEAD_DIM)),
        'ev_k_norm': gain((N_EVEN, SB_HEAD_DIM)),
        'ev_w_out': uni((N_EVEN, SSD_INNER + SB_INNER, D_MODEL), out_std * (SSD_INNER + SB_INNER) ** -0.5),
        'od_w_in': uni((N_ODD, D_MODEL, OD_IN), D_MODEL ** -0.5),
        'od_conv_w': uni((N_ODD, CONV_K, GDN_CONV_DIM), CONV_K ** -0.5),
        'od_dt_bias': dt_bias((N_ODD, GDN_V_HEADS)),
        'od_a_log': a_log((N_ODD, GDN_V_HEADS)),
        'od_out_norm': gain((N_ODD, GDN_V_DIM)),
        'od_w_out': uni((N_ODD, GDN_VAL, D_MODEL), out_std * GDN_VAL ** -0.5),
        'moe_w_router': uni((DEPTH, D_MODEL, N_EXPERTS), D_MODEL ** -0.5),
        'moe_b_router': small((DEPTH, N_EXPERTS)),
        'moe_w_gate_up': uni((DEPTH, N_EXPERTS, D_MODEL, 2 * EXPERT_FF), D_MODEL ** -0.5),
        'moe_b_gate_up': small((DEPTH, N_EXPERTS, 2 * EXPERT_FF)),
        'moe_w_down': uni((DEPTH, N_EXPERTS, EXPERT_FF, D_MODEL), out_std * EXPERT_FF ** -0.5),
        'moe_b_down': small((DEPTH, N_EXPERTS, D_MODEL)),
    }


def reference(x, mix_norm, ffn_norm,
              ev_w_in, ev_conv_w, ev_conv_b, ev_dt_bias, ev_a_log, ev_d_skip, ev_ssd_norm,
              ev_q_norm, ev_k_norm, ev_w_out,
              od_w_in, od_conv_w, od_dt_bias, od_a_log, od_out_norm, od_w_out,
              moe_w_router, moe_b_router, moe_w_gate_up, moe_b_gate_up, moe_w_down, moe_b_down):
    for layer in range(DEPTH):
        i = layer // 2
        h = rms_norm(x, mix_norm[layer]).astype(x.dtype)
        if layer % 2 == 0:
            x = x + ssd_stickbreak_mixer(h, ev_w_in[i], ev_conv_w[i], ev_conv_b[i], ev_dt_bias[i],
                                         ev_a_log[i], ev_d_skip[i], ev_ssd_norm[i], ev_q_norm[i],
                                         ev_k_norm[i], ev_w_out[i])
        else:
            x = x + gated_deltanet_mixer(h, od_w_in[i], od_conv_w[i], od_dt_bias[i], od_a_log[i],
                                         od_out_norm[i], od_w_out[i])
        h = rms_norm(x, ffn_norm[layer]).astype(x.dtype)
        x = x + moe_ffn(h, moe_w_router[layer], moe_b_router[layer], moe_w_gate_up[layer],
                        moe_b_gate_up[layer], moe_w_down[layer], moe_b_down[layer])
    return x
```

```python
import functools
import math

import jax
import jax.numpy as jnp
from jax import lax
from jax.experimental import pallas as pl
from jax.experimental.pallas import tpu as pltpu

F32 = jnp.float32
BF16 = jnp.bfloat16
I32 = jnp.int32
U32 = jnp.uint32

D_MODEL = 2048
RMS_EPS = 1e-6
CONV_K = 4

SSD_HEADS = 32
SSD_HEAD_DIM = 64
SSD_INNER = 2048
SSD_GROUPS = 4
SSD_HPG = 8
SSD_STATE = 128
SSD_CHUNK = 128
SSD_CONV_DIM = 3072
SB_HEADS = 16
SB_HEAD_DIM = 128
SB_INNER = 2048
SB_BLOCK = 128

GDN_K_HEADS = 16
GDN_V_HEADS = 32
GDN_K_DIM = 128
GDN_V_DIM = 128
GDN_KEY = 2048
GDN_VAL = 4096
GDN_CONV_DIM = 8192
GDN_CHUNK = 64

N_EXPERTS = 32
TOP_K = 4
EXPERT_FF = 1024
SWIGLU_LIMIT = 7.0
SWIGLU_ALPHA = 1.702

LANES = 128
VMEM_LIMIT = 56 * 1024 * 1024

TILE_M = 1024
SUB_M = 256
N_SUB = TILE_M // SUB_M
MAX_TILES = (8192 * TOP_K) // TILE_M + N_EXPERTS
FF_CHUNK = 256
N_FF_CHUNKS = EXPERT_FF // FF_CHUNK
HALF_D = D_MODEL // 2


def _cparams(sem):
    return pltpu.CompilerParams(dimension_semantics=sem, vmem_limit_bytes=VMEM_LIMIT)


def _norm_matmul_kernel(x_ref, nw_ref, w_ref, o_ref, h_ref):
    @pl.when(pl.program_id(1) == 0)
    def _():
        rows = 128

        def body(r, c):
            sl = pl.ds(pl.multiple_of(r * rows, rows), rows)
            x = x_ref[sl, :]
            ms = jnp.mean(x * x, axis=-1, keepdims=True)
            h_ref[sl, :] = (x * lax.rsqrt(ms + RMS_EPS) * nw_ref[...]).astype(BF16)
            return c

        lax.fori_loop(0, x_ref.shape[0] // rows, body, 0)

    o_ref[...] = jnp.dot(h_ref[...], w_ref[...], preferred_element_type=F32)


def norm_matmul(x2, nw, w_bf, tm=1024, tn=512):
    t, d = x2.shape
    n = w_bf.shape[1]
    return pl.pallas_call(
        _norm_matmul_kernel,
        grid=(t // tm, n // tn),
        in_specs=[pl.BlockSpec((tm, d), lambda i, j: (i, 0)),
                  pl.BlockSpec((1, d), lambda i, j: (0, 0)),
                  pl.BlockSpec((d, tn), lambda i, j: (0, j))],
        out_specs=pl.BlockSpec((tm, tn), lambda i, j: (i, j)),
        out_shape=jax.ShapeDtypeStruct((t, n), F32),
        scratch_shapes=[pltpu.VMEM((tm, d), BF16)],
        compiler_params=_cparams(("parallel", "arbitrary")),
        name="norm_matmul",
    )(x2, nw.reshape(1, d), w_bf)


def _matmul_residual_kernel(*refs):
    n_in = (len(refs) - 2) // 2
    x_ref, o_ref = refs[2 * n_in], refs[2 * n_in + 1]
    acc = x_ref[...]
    for i in range(n_in):
        acc = acc + jnp.dot(refs[i][...], refs[n_in + i][...], preferred_element_type=F32)
    o_ref[...] = acc


def matmul_residual(ys, ws, x2, tm=1024, tn=512):
    t, n = x2.shape
    in_specs = [pl.BlockSpec((tm, y.shape[1]), lambda i, j: (i, 0)) for y in ys]
    in_specs += [pl.BlockSpec((w.shape[0], tn), lambda i, j: (0, j)) for w in ws]
    in_specs += [pl.BlockSpec((tm, tn), lambda i, j: (i, j))]
    return pl.pallas_call(
        _matmul_residual_kernel,
        grid=(t // tm, n // tn),
        in_specs=in_specs,
        out_specs=pl.BlockSpec((tm, tn), lambda i, j: (i, j)),
        out_shape=jax.ShapeDtypeStruct((t, n), F32),
        compiler_params=_cparams(("parallel", "parallel")),
        name="matmul_residual",
    )(*ys, *ws, x2)


ROUTER_TB = 256


def _router_kernel(x_ref, nw_ref, wr_ref, br_ref, hp_ref, ri_ref, rf_ref, cnt_ref, carry_ref):
    tb = ROUTER_TB

    @pl.when(pl.program_id(0) == 0)
    def _():
        carry_ref[...] = jnp.zeros_like(carry_ref)

    x = x_ref[...]
    ms = jnp.mean(x * x, axis=-1, keepdims=True)
    h = x * lax.rsqrt(ms + RMS_EPS) * nw_ref[...]

    lo = h[:, :HALF_D].astype(BF16).astype(F32)
    hi = h[:, HALF_D:].astype(BF16).astype(F32)
    lo_b = lax.shift_right_logical(lax.bitcast_convert_type(lo, U32), jnp.uint32(16))
    hi_b = lax.bitcast_convert_type(hi, U32) & jnp.uint32(0xFFFF0000)
    hp_ref[...] = hi_b | lo_b

    logits = jnp.dot(h, wr_ref[...], preferred_element_type=F32,
                     precision=lax.Precision.HIGHEST) + br_ref[...]
    lane = lax.broadcasted_iota(I32, (tb, LANES), 1)
    vals, idxs = [], []
    cur = logits
    for _ in range(TOP_K):
        m = jnp.max(cur, axis=-1, keepdims=True)
        idx = jnp.min(jnp.where(cur == m, lane, LANES), axis=-1, keepdims=True)
        vals.append(m)
        idxs.append(idx)
        cur = jnp.where(lane == idx, -jnp.inf, cur)

    exps = [jnp.exp(v - vals[0]) for v in vals]
    denom = exps[0] + exps[1] + exps[2] + exps[3]
    gates = [e / denom for e in exps]

    onehot = jnp.zeros((tb, LANES), F32)
    for idx in idxs:
        onehot = onehot + (lane == idx).astype(F32)
    r_i = lax.broadcasted_iota(I32, (tb, tb), 0)
    c_i = lax.broadcasted_iota(I32, (tb, tb), 1)
    tri = (r_i > c_i).astype(BF16)
    carry = carry_ref[0:1, :]
    cum = jnp.dot(tri, onehot.astype(BF16), preferred_element_type=F32) + carry
    carry_new = carry + jnp.sum(onehot, axis=0, keepdims=True)
    carry_ref[...] = jnp.broadcast_to(carry_new, carry_ref.shape)
    cnt_ref[...] = jnp.broadcast_to(carry_new, cnt_ref.shape)

    ri = jnp.zeros((tb, LANES), I32)
    rf = jnp.zeros((tb, LANES), F32)
    for k in range(TOP_K):
        rank = jnp.sum(jnp.where(lane == idxs[k], cum, 0.0), axis=-1, keepdims=True).astype(I32)
        ri = jnp.where(lane == k, idxs[k], ri)
        ri = jnp.where(lane == TOP_K + k, rank, ri)
        rf = jnp.where(lane == k, gates[k], rf)
    ri_ref[...] = ri
    rf_ref[...] = rf


def router(x2, nw, w_router, b_router):
    t, d = x2.shape
    tb = ROUTER_TB
    wr = jnp.zeros((d, LANES), F32).at[:, :N_EXPERTS].set(w_router)
    br = jnp.full((1, LANES), -jnp.inf, F32).at[0, :N_EXPERTS].set(b_router)
    return pl.pallas_call(
        _router_kernel,
        grid=(t // tb,),
        in_specs=[pl.BlockSpec((tb, d), lambda i: (i, 0)),
                  pl.BlockSpec((1, d), lambda i: (0, 0)),
                  pl.BlockSpec((d, LANES), lambda i: (0, 0)),
                  pl.BlockSpec((1, LANES), lambda i: (0, 0))],
        out_specs=[pl.BlockSpec((tb, HALF_D), lambda i: (i, 0)),
                   pl.BlockSpec((tb, LANES), lambda i: (i, 0)),
                   pl.BlockSpec((tb, LANES), lambda i: (i, 0)),
                   pl.BlockSpec((8, LANES), lambda i: (0, 0))],
        out_shape=[jax.ShapeDtypeStruct((t, HALF_D), U32),
                   jax.ShapeDtypeStruct((t, LANES), I32),
                   jax.ShapeDtypeStruct((t, LANES), F32),
                   jax.ShapeDtypeStruct((8, LANES), F32)],
        scratch_shapes=[pltpu.VMEM((8, LANES), F32)],
        compiler_params=_cparams(("arbitrary",)),
        name="moe_router",
    )(x2, nw.reshape(1, d), wr, br)


DISPATCH_TB = 256


def _dispatch_kernel(dest_ref, hp_ref, xs_init_ref, xs_ref, sem):
    del xs_init_ref
    tb = DISPATCH_TB

    def body(r, c):
        for k in range(TOP_K):
            d = dest_ref[0, 0, r * TOP_K + k]
            pltpu.make_async_copy(hp_ref.at[pl.ds(r, 1)], xs_ref.at[pl.ds(d, 1)], sem).start()
        return c

    lax.fori_loop(0, tb, body, 0)
    for k in range(TOP_K):
        pltpu.make_async_copy(hp_ref, xs_ref.at[pl.ds(0, tb)], sem).wait()


def dispatch(hp, dest_blocks, n_rows):
    t, w = hp.shape
    tb = DISPATCH_TB
    return pl.pallas_call(
        _dispatch_kernel,
        grid=(t // tb,),
        in_specs=[pl.BlockSpec((1, 1, tb * TOP_K), lambda i: (i, 0, 0), memory_space=pltpu.SMEM),
                  pl.BlockSpec((tb, w), lambda i: (i, 0)),
                  pl.BlockSpec(memory_space=pl.ANY)],
        out_specs=pl.BlockSpec(memory_space=pl.ANY),
        out_shape=jax.ShapeDtypeStruct((n_rows, w), hp.dtype),
        scratch_shapes=[pltpu.SemaphoreType.DMA],
        input_output_aliases={2: 0},
        compiler_params=_cparams(("arbitrary",)),
        name="moe_dispatch",
    )(dest_blocks, hp, jnp.zeros((n_rows, w), hp.dtype))


def _moe_ffn_kernel(te_ref, tb_ref, nr_ref, x_ref, wg_ref, wu_ref, wd_ref, bg_ref, bu_ref, bd_ref, o_ref,
                    xlo_ref, xhi_ref, wgu_ref, wdb_ref):
    j = pl.program_id(0)
    c = pl.program_id(1)
    nrows = nr_ref[j]

    @pl.when(nrows > 0)
    def _():
        @pl.when(c == 0)
        def _():
            for s in range(N_SUB):
                rows = pl.ds(s * SUB_M, SUB_M)
                w = x_ref[rows, :]
                lo = lax.bitcast_convert_type(lax.shift_left(w, jnp.uint32(16)), F32)
                hi = lax.bitcast_convert_type(w & jnp.uint32(0xFFFF0000), F32)
                xlo_ref[rows, :] = lo.astype(BF16)
                xhi_ref[rows, :] = hi.astype(BF16)

        def cast_body(r, carry):
            sl = pl.ds(pl.multiple_of(r * 256, 256), 256)
            wgu_ref[sl, 0:FF_CHUNK] = wg_ref[sl, :].astype(BF16)
            wgu_ref[sl, FF_CHUNK:2 * FF_CHUNK] = wu_ref[sl, :].astype(BF16)
            return carry

        lax.fori_loop(0, D_MODEL // 256, cast_body, 0)
        wdb_ref[...] = wd_ref[...].astype(BF16)

        for s in range(N_SUB):
            rows = pl.ds(s * SUB_M, SUB_M)

            @pl.when(s * SUB_M < nrows)
            def _():
                gu = jnp.dot(xlo_ref[rows, :], wgu_ref[0:HALF_D, :], preferred_element_type=F32)
                gu = gu + jnp.dot(xhi_ref[rows, :], wgu_ref[HALF_D:D_MODEL, :], preferred_element_type=F32)
                g = jnp.minimum(gu[:, 0:FF_CHUNK] + bg_ref[...], SWIGLU_LIMIT)
                u = jnp.clip(gu[:, FF_CHUNK:2 * FF_CHUNK] + bu_ref[...], -SWIGLU_LIMIT, SWIGLU_LIMIT)
                act = (u + 1.0) * g * jax.nn.sigmoid(SWIGLU_ALPHA * g)
                contrib = jnp.dot(act.astype(BF16), wdb_ref[...], preferred_element_type=F32)

                @pl.when(c == 0)
                def _():
                    o_ref[rows, :] = contrib + bd_ref[...]

                @pl.when(c > 0)
                def _():
                    o_ref[rows, :] += contrib

            @pl.when(jnp.logical_and(s * SUB_M >= nrows, c == 0))
            def _():
                o_ref[rows, :] = jnp.zeros((SUB_M, D_MODEL), F32)

    @pl.when(jnp.logical_and(nrows == 0, c == 0))
    def _():
        o_ref[...] = jnp.zeros(o_ref.shape, F32)


def moe_ffn(xs, tile_expert, tile_block, tile_rows, w_gate_up, b_gate_up, w_down, b_down):
    n_rows = xs.shape[0]
    bgu = b_gate_up.reshape(N_EXPERTS, 1, 2 * EXPERT_FF)
    bd = b_down.reshape(N_EXPERTS, 1, D_MODEL)
    grid_spec = pltpu.PrefetchScalarGridSpec(
        num_scalar_prefetch=3,
        grid=(MAX_TILES, N_FF_CHUNKS),
        in_specs=[
            pl.BlockSpec((TILE_M, HALF_D), lambda j, c, te, tb, nr: (tb[j], 0)),
            pl.BlockSpec((None, D_MODEL, FF_CHUNK), lambda j, c, te, tb, nr: (te[j], 0, c)),
            pl.BlockSpec((None, D_MODEL, FF_CHUNK), lambda j, c, te, tb, nr: (te[j], 0, N_FF_CHUNKS + c)),
            pl.BlockSpec((None, FF_CHUNK, D_MODEL), lambda j, c, te, tb, nr: (te[j], c, 0)),
            pl.BlockSpec((None, 1, FF_CHUNK), lambda j, c, te, tb, nr: (te[j], 0, c)),
            pl.BlockSpec((None, 1, FF_CHUNK), lambda j, c, te, tb, nr: (te[j], 0, N_FF_CHUNKS + c)),
            pl.BlockSpec((None, 1, D_MODEL), lambda j, c, te, tb, nr: (te[j], 0, 0)),
        ],
        out_specs=pl.BlockSpec((TILE_M, D_MODEL), lambda j, c, te, tb, nr: (j, 0)),
        scratch_shapes=[pltpu.VMEM((TILE_M, HALF_D), BF16),
                        pltpu.VMEM((TILE_M, HALF_D), BF16),
                        pltpu.VMEM((D_MODEL, 2 * FF_CHUNK), BF16),
                        pltpu.VMEM((FF_CHUNK, D_MODEL), BF16)],
    )
    return pl.pallas_call(
        _moe_ffn_kernel,
        grid_spec=grid_spec,
        out_shape=jax.ShapeDtypeStruct((n_rows, D_MODEL), F32),
        compiler_params=_cparams(("arbitrary", "arbitrary")),
        name="moe_ffn",
    )(tile_expert, tile_block, tile_rows, xs, w_gate_up, w_gate_up, w_down, bgu, bgu, bd)


COMBINE_TB = 256


def _combine_kernel(dest_ref, x_ref, g_ref, ys_ref, o_ref, buf_ref, sem):
    tb = COMBINE_TB

    def body(r, c):
        for k in range(TOP_K):
            d = dest_ref[0, 0, r * TOP_K + k]
            pltpu.make_async_copy(ys_ref.at[pl.ds(d, 1)], buf_ref.at[k, pl.ds(r, 1)], sem).start()
        return c

    lax.fori_loop(0, tb, body, 0)
    for k in range(TOP_K):
        pltpu.make_async_copy(ys_ref.at[pl.ds(0, tb)], buf_ref.at[k], sem).wait()
    acc = x_ref[...]
    g = g_ref[...]
    for k in range(TOP_K):
        acc = acc + g[:, k:k + 1] * buf_ref[k]
    o_ref[...] = acc


def combine(x2, gates, ys, dest_blocks):
    t, d = x2.shape
    tb = COMBINE_TB
    return pl.pallas_call(
        _combine_kernel,
        grid=(t // tb,),
        in_specs=[pl.BlockSpec((1, 1, tb * TOP_K), lambda i: (i, 0, 0), memory_space=pltpu.SMEM),
                  pl.BlockSpec((tb, d), lambda i: (i, 0)),
                  pl.BlockSpec((tb, LANES), lambda i: (i, 0)),
                  pl.BlockSpec(memory_space=pl.ANY)],
        out_specs=pl.BlockSpec((tb, d), lambda i: (i, 0)),
        out_shape=jax.ShapeDtypeStruct((t, d), F32),
        scratch_shapes=[pltpu.VMEM((TOP_K, tb, d), F32), pltpu.SemaphoreType.DMA],
        compiler_params=_cparams(("arbitrary",)),
        name="moe_combine",
    )(dest_blocks, x2, gates, ys)


def moe_layer(x2, nw, w_router, b_router, w_gate_up, b_gate_up, w_down, b_down):
    t = x2.shape[0]
    hp, ri, rf, cnt = router(x2, nw, w_router, b_router)
    idx = ri[:, 0:TOP_K]
    rank = ri[:, TOP_K:2 * TOP_K]
    counts = cnt[0, :N_EXPERTS].astype(I32)

    tiles_e = (counts + TILE_M - 1) // TILE_M
    tile_end = jnp.cumsum(tiles_e)
    tile_start = tile_end - tiles_e
    n_tiles = tile_end[-1]
    dest = (tile_start * TILE_M)[idx] + rank
    j = jnp.arange(MAX_TILES, dtype=I32)
    jc = jnp.minimum(j, n_tiles - 1)
    te = jnp.sum((jc[:, None] >= tile_end[None, :]).astype(I32), axis=1)
    rows_left = counts[te] - (jc - tile_start[te]) * TILE_M
    tile_rows = jnp.where(j < n_tiles, jnp.clip(rows_left, 0, TILE_M), 0).astype(I32)
    dest_blocks = dest.reshape(t // DISPATCH_TB, 1, DISPATCH_TB * TOP_K).astype(I32)

    xs = dispatch(hp, dest_blocks, MAX_TILES * TILE_M)
    ys = moe_ffn(xs, te.astype(I32), jc.astype(I32), tile_rows, w_gate_up, b_gate_up, w_down, b_down)
    return combine(x2, rf, ys, dest_blocks)


def _rms_norm(x, w):
    return x * lax.rsqrt(jnp.mean(x * x, axis=-1, keepdims=True) + RMS_EPS) * w


def _l2_norm(x):
    return x * lax.rsqrt(jnp.sum(x * x, axis=-1, keepdims=True) + RMS_EPS)


def _causal_dwconv(x, w):
    c = x.shape[-1]
    return lax.conv_general_dilated(
        x, w[:, None, :], window_strides=(1,), padding=((CONV_K - 1, 0),),
        dimension_numbers=('NWC', 'WIO', 'NWC'), feature_group_count=c)


def _ssd_chunked_scan(xbc, dt_raw, dt_bias, a_log, d_skip):
    bsz, seq, _ = xbc.shape
    nc = seq // SSD_CHUNK
    xs, bm, cm = jnp.split(xbc, (SSD_INNER, SSD_INNER + SSD_GROUPS * SSD_STATE), axis=-1)
    xs = xs.reshape(bsz, nc, SSD_CHUNK, SSD_GROUPS, SSD_HPG, SSD_HEAD_DIM)
    bm = bm.reshape(bsz, nc, SSD_CHUNK, SSD_GROUPS, SSD_STATE)
    cm = cm.reshape(bsz, nc, SSD_CHUNK, SSD_GROUPS, SSD_STATE)
    dt = jax.nn.softplus(dt_raw + dt_bias)
    dt = dt.reshape(bsz, nc, SSD_CHUNK, SSD_GROUPS, SSD_HPG)
    log_a = -jnp.exp(a_log).reshape(SSD_GROUPS, SSD_HPG) * dt
    a_cs = jnp.cumsum(jnp.transpose(log_a, (0, 1, 3, 4, 2)), axis=-1)
    xdt = xs * dt[..., None]
    causal = jnp.tril(jnp.ones((SSD_CHUNK, SSD_CHUNK), bool))
    decay = jnp.exp(jnp.where(causal, a_cs[..., :, None] - a_cs[..., None, :], -jnp.inf))
    scores = jnp.einsum('bclgn,bcsgn->bcgls', cm, bm)[:, :, :, None] * decay
    y_diag = jnp.einsum('bcgrls,bcsgrp->bclgrp', scores, xdt)
    to_end = jnp.moveaxis(jnp.exp(a_cs[..., -1:] - a_cs), -1, 2)[..., None]
    states = jnp.einsum('bcsgn,bcsgrp->bcgrpn', bm, xdt * to_end)
    chunk_decay = jnp.exp(a_cs[..., -1])

    def carry_state(h, inp):
        st, dec = inp
        return h * dec[..., None, None] + st, h

    h0 = jnp.zeros_like(states[:, 0])
    _, h_in = lax.scan(carry_state, h0, (jnp.moveaxis(states, 1, 0), jnp.moveaxis(chunk_decay, 1, 0)))
    h_in = jnp.moveaxis(h_in, 0, 1)
    from_start = jnp.moveaxis(jnp.exp(a_cs), -1, 2)[..., None]
    y_off = jnp.einsum('bclgn,bcgrpn->bclgrp', cm, h_in) * from_start
    y = y_diag + y_off + xs * d_skip.reshape(SSD_GROUPS, SSD_HPG, 1)
    return y.reshape(bsz, seq, SSD_INNER)


def _stick_breaking_attention(q, k, v):
    seq = q.shape[1]
    scale = SB_HEAD_DIM ** -0.5
    local = jnp.arange(SB_BLOCK)
    outs = []
    for start in range(0, seq, SB_BLOCK):
        stop = start + SB_BLOCK
        logits = jnp.einsum('bqhd,bkhd->bhqk', q[:, start:stop], k[:, :stop]) * scale
        strict = jnp.arange(stop)[None, :] < (start + local)[:, None]
        log_keep = jnp.where(strict, jax.nn.log_sigmoid(-logits), 0.0)
        between = lax.cumsum(log_keep, axis=3, reverse=True) - log_keep
        weights = jnp.where(strict, jnp.exp(jax.nn.log_sigmoid(logits) + between), 0.0)
        outs.append(jnp.einsum('bhqk,bkhd->bqhd', weights, v[:, :stop]))
    return jnp.concatenate(outs, axis=1)


def even_mixer(x, nw, w_in, conv_w, conv_b, dt_bias, a_log, d_skip, ssd_norm, q_norm, k_norm, w_out):
    bsz, seq, d = x.shape
    x2 = x.reshape(bsz * seq, d)
    o2 = SSD_INNER + SSD_CONV_DIM
    o3 = o2 + SSD_HEADS
    w_cat = jnp.concatenate(
        [w_in[:, :o2], w_in[:, o3:], w_in[:, o2:o3], jnp.zeros((d, 512 - SSD_HEADS), F32)], axis=1).astype(BF16)
    proj = norm_matmul(x2, nw, w_cat).reshape(bsz, seq, -1)
    z = proj[..., :SSD_INNER]
    xbc = proj[..., SSD_INNER:o2]
    q = proj[..., o2:o2 + SB_INNER]
    k = proj[..., o2 + SB_INNER:o2 + 2 * SB_INNER]
    v = proj[..., o2 + 2 * SB_INNER:o2 + 3 * SB_INNER]
    dt_raw = proj[..., o2 + 3 * SB_INNER:o2 + 3 * SB_INNER + SSD_HEADS]

    xbc = jax.nn.silu(_causal_dwconv(xbc, conv_w) + conv_b)
    y_ssd = _ssd_chunked_scan(xbc, dt_raw, dt_bias, a_log, d_skip)
    y_ssd = _rms_norm(y_ssd * jax.nn.silu(z), ssd_norm)
    qh = _rms_norm(q.reshape(bsz, seq, SB_HEADS, SB_HEAD_DIM), q_norm)
    kh = _rms_norm(k.reshape(bsz, seq, SB_HEADS, SB_HEAD_DIM), k_norm)
    vh = v.reshape(bsz, seq, SB_HEADS, SB_HEAD_DIM)
    y_sb = _stick_breaking_attention(qh, kh, vh).reshape(bsz, seq, SB_INNER)

    w_out_bf = w_out.astype(BF16)
    out = matmul_residual(
        [y_ssd.reshape(bsz * seq, SSD_INNER).astype(BF16), y_sb.reshape(bsz * seq, SB_INNER).astype(BF16)],
        [w_out_bf[:SSD_INNER], w_out_bf[SSD_INNER:]], x2)
    return out.reshape(bsz, seq, d)


def _gated_delta_rule(q, k, v, g, beta):
    bsz, seq, nh, dk = q.shape
    dv = v.shape[-1]
    nc = seq // GDN_CHUNK

    def chunks(t):
        return t.reshape(bsz, nc, GDN_CHUNK, nh, -1).transpose(0, 3, 1, 2, 4)

    qc = chunks(q * dk ** -0.5)
    kc = chunks(k)
    vc = chunks(v)
    bc = chunks(beta[..., None])
    g_cs = jnp.cumsum(chunks(g[..., None])[..., 0], axis=-1)
    incl = jnp.tril(jnp.ones((GDN_CHUNK, GDN_CHUNK), bool))
    strict = jnp.tril(jnp.ones((GDN_CHUNK, GDN_CHUNK), bool), -1)
    decay = jnp.exp(jnp.where(incl, g_cs[..., :, None] - g_cs[..., None, :], -jnp.inf))
    kb = kc * bc
    a = jnp.where(strict, jnp.einsum('bhncd,bhnsd->bhncs', kb, kc) * decay, 0.0)
    eye = jnp.eye(GDN_CHUNK, dtype=F32)
    t = lax.linalg.triangular_solve(a + eye, jnp.broadcast_to(eye, a.shape), left_side=True,
                                    lower=True, unit_diagonal=True)
    u = t @ (vc * bc)
    w = t @ (kb * jnp.exp(g_cs)[..., None])
    qk = jnp.einsum('bhncd,bhnsd->bhncs', qc, kc) * decay
    q_decayed = qc * jnp.exp(g_cs)[..., None]
    k_to_end = kc * jnp.exp(g_cs[..., -1:] - g_cs)[..., None]
    chunk_decay = jnp.exp(g_cs[..., -1])

    def step(s, inp):
        q_i, k_i, u_i, w_i, qk_i, dec_i = inp
        v_new = u_i - w_i @ s
        o_i = q_i @ s + qk_i @ v_new
        s = s * dec_i[..., None, None] + jnp.swapaxes(k_i, -1, -2) @ v_new
        return s, o_i

    s0 = jnp.zeros((bsz, nh, dk, dv), F32)
    xs = (jnp.moveaxis(q_decayed, 2, 0), jnp.moveaxis(k_to_end, 2, 0), jnp.moveaxis(u, 2, 0),
          jnp.moveaxis(w, 2, 0), jnp.moveaxis(qk, 2, 0), jnp.moveaxis(chunk_decay, 2, 0))
    _, o = lax.scan(step, s0, xs)
    return o.transpose(1, 0, 3, 2, 4).reshape(bsz, seq, nh, dv)


def odd_mixer(x, nw, w_in, conv_w, dt_bias, a_log, out_norm, w_out):
    bsz, seq, d = x.shape
    x2 = x.reshape(bsz * seq, d)
    n_in = w_in.shape[1]
    w_cat = jnp.concatenate([w_in, jnp.zeros((d, 12800 - n_in), F32)], axis=1).astype(BF16)
    proj = norm_matmul(x2, nw, w_cat).reshape(bsz, seq, -1)
    o1 = GDN_CONV_DIM
    o2 = o1 + GDN_VAL
    qkv = proj[..., :o1]
    z = proj[..., o1:o2]
    b_raw = proj[..., o2:o2 + GDN_V_HEADS]
    a_raw = proj[..., o2 + GDN_V_HEADS:o2 + 2 * GDN_V_HEADS]

    qkv = jax.nn.silu(_causal_dwconv(qkv, conv_w))
    q, k, v = jnp.split(qkv, (GDN_KEY, 2 * GDN_KEY), axis=-1)
    rep = GDN_V_HEADS // GDN_K_HEADS
    q = jnp.repeat(_l2_norm(q.reshape(bsz, seq, GDN_K_HEADS, GDN_K_DIM)), rep, axis=2)
    k = jnp.repeat(_l2_norm(k.reshape(bsz, seq, GDN_K_HEADS, GDN_K_DIM)), rep, axis=2)
    v = v.reshape(bsz, seq, GDN_V_HEADS, GDN_V_DIM)
    beta = jax.nn.sigmoid(b_raw)
    g = -jnp.exp(a_log) * jax.nn.softplus(a_raw + dt_bias)
    o = _gated_delta_rule(q, k, v, g, beta)
    o = _rms_norm(o, out_norm) * jax.nn.silu(z.reshape(bsz, seq, GDN_V_HEADS, GDN_V_DIM))
    out = matmul_residual([o.reshape(bsz * seq, GDN_VAL).astype(BF16)], [w_out.astype(BF16)], x2)
    return out.reshape(bsz, seq, d)


def kernel(x, mix_norm, ffn_norm, ev_w_in, ev_conv_w, ev_conv_b, ev_dt_bias, ev_a_log, ev_d_skip, ev_ssd_norm, ev_q_norm, ev_k_norm, ev_w_out, od_w_in, od_conv_w, od_dt_bias, od_a_log, od_out_norm, od_w_out, moe_w_router, moe_b_router, moe_w_gate_up, moe_b_gate_up, moe_w_down, moe_b_down):
    bsz, seq, d = x.shape
    depth = mix_norm.shape[0]
    for layer in range(depth):
        i = layer // 2
        if layer % 2 == 0:
            x = even_mixer(x, mix_norm[layer], ev_w_in[i], ev_conv_w[i], ev_conv_b[i], ev_dt_bias[i],
                           ev_a_log[i], ev_d_skip[i], ev_ssd_norm[i], ev_q_norm[i], ev_k_norm[i], ev_w_out[i])
        else:
            x = odd_mixer(x, mix_norm[layer], od_w_in[i], od_conv_w[i], od_dt_bias[i], od_a_log[i],
                          od_out_norm[i], od_w_out[i])
        x2 = moe_layer(x.reshape(bsz * seq, d), ffn_norm[layer], moe_w_router[layer], moe_b_router[layer],
                       moe_w_gate_up[layer], moe_b_gate_up[layer], moe_w_down[layer], moe_b_down[layer])
        x = x2.reshape(bsz, seq, d)
    return x
```

```python
import jax
import jax.numpy as jnp
from jax import lax
from jax.experimental import pallas as pl
from jax.experimental.pallas import tpu as pltpu

F32 = jnp.float32
BF16 = jnp.bfloat16
I32 = jnp.int32
U32 = jnp.uint32

D_MODEL = 2048
RMS_EPS = 1e-6
CONV_K = 4

SSD_HEADS = 32
SSD_HEAD_DIM = 64
SSD_INNER = 2048
SSD_GROUPS = 4
SSD_HPG = 8
SSD_STATE = 128
SSD_CHUNK = 128
SSD_GW = SSD_HPG * SSD_HEAD_DIM
SSD_BC = 2 * SSD_GROUPS * SSD_STATE

SB_HEADS = 16
SB_HEAD_DIM = 128
SB_INNER = 2048
SB_TQ = 256
SB_TK = 256
SB_EXIT = -104.0

GDN_K_HEADS = 16
GDN_V_HEADS = 32
GDN_DIM = 128
GDN_KEY = GDN_K_HEADS * GDN_DIM
GDN_VAL = GDN_V_HEADS * GDN_DIM
GDN_CHUNK = 64
GDN_RB = 256
GDN_A_LANE = GDN_V_HEADS

N_EXPERTS = 32
TOP_K = 4
EXPERT_FF = 1024
SWIGLU_LIMIT = 7.0
SWIGLU_ALPHA = 1.702

LANES = 128
SUBLANES = 8
VMEM_LIMIT = 56 * 1024 * 1024

TILE_M = 1024
SUB_M = 256
N_SUB = TILE_M // SUB_M
FF_CHUNK = 256
N_FF_CHUNKS = EXPERT_FF // FF_CHUNK
HALF_D = D_MODEL // 2
PROJ_TN = 512

HALO = SUBLANES


def _cparams(sem):
    return pltpu.CompilerParams(dimension_semantics=sem, vmem_limit_bytes=VMEM_LIMIT)


def _dot(a, b):
    return jnp.dot(a, b, preferred_element_type=F32)


def _dot_nt(a, b):
    return lax.dot_general(a, b, (((1,), (1,)), ((), ())), preferred_element_type=F32)


def _split_bf16(x):
    hi = x.astype(BF16)
    lo = (x - hi.astype(F32)).astype(BF16)
    return hi, lo


def _dot_hilo(x, m_bf):
    hi, lo = _split_bf16(x)
    return _dot(hi, m_bf) + _dot(lo, m_bf)


def _dot_hilo_lhs(m_bf, x):
    hi, lo = _split_bf16(x)
    return _dot(m_bf, hi) + _dot(m_bf, lo)


def _softplus(x):
    return jnp.maximum(x, 0.0) + jnp.log1p(jnp.exp(-jnp.abs(x)))


def _silu(x):
    return x * jax.nn.sigmoid(x)


def _norm_matmul_kernel(x_ref, nw_ref, w_ref, o_ref, h_ref):
    @pl.when(pl.program_id(1) == 0)
    def _():
        rows = 128

        def body(r, c):
            sl = pl.ds(pl.multiple_of(r * rows, rows), rows)
            x = x_ref[sl, :]
            ms = jnp.mean(x * x, axis=-1, keepdims=True)
            h_ref[sl, :] = (x * lax.rsqrt(ms + RMS_EPS) * nw_ref[...]).astype(BF16)
            return c

        lax.fori_loop(0, x_ref.shape[0] // rows, body, 0)

    o_ref[...] = _dot(h_ref[...], w_ref[...])


def norm_matmul(x2, nw, w_bf, tm=1024, tn=PROJ_TN):
    t, d = x2.shape
    n = w_bf.shape[1]
    return pl.pallas_call(
        _norm_matmul_kernel,
        grid=(t // tm, n // tn),
        in_specs=[pl.BlockSpec((tm, d), lambda i, j: (i, 0)),
                  pl.BlockSpec((1, d), lambda i, j: (0, 0)),
                  pl.BlockSpec((d, tn), lambda i, j: (0, j))],
        out_specs=pl.BlockSpec((tm, tn), lambda i, j: (i, j)),
        out_shape=jax.ShapeDtypeStruct((t, n), F32),
        scratch_shapes=[pltpu.VMEM((tm, d), BF16)],
        compiler_params=_cparams(("parallel", "arbitrary")),
        name="norm_matmul",
    )(x2, nw.reshape(1, d), w_bf)


def _matmul_residual_kernel(*refs):
    n_in = (len(refs) - 2) // 2
    x_ref, o_ref = refs[2 * n_in], refs[2 * n_in + 1]
    acc = x_ref[...]
    for i in range(n_in):
        acc = acc + _dot(refs[i][...], refs[n_in + i][...])
    o_ref[...] = acc


def matmul_residual(ys, ws, x2, tm=1024, tn=512):
    t, n = x2.shape
    in_specs = [pl.BlockSpec((tm, y.shape[1]), lambda i, j: (i, 0)) for y in ys]
    in_specs += [pl.BlockSpec((w.shape[0], tn), lambda i, j: (0, j)) for w in ws]
    in_specs += [pl.BlockSpec((tm, tn), lambda i, j: (i, j))]
    return pl.pallas_call(
        _matmul_residual_kernel,
        grid=(t // tm, n // tn),
        in_specs=in_specs,
        out_specs=pl.BlockSpec((tm, tn), lambda i, j: (i, j)),
        out_shape=jax.ShapeDtypeStruct((t, n), F32),
        compiler_params=_cparams(("parallel", "parallel")),
        name="matmul_residual",
    )(*ys, *ws, x2)


def _sb_kernel(q_ref, k_ref, v_ref, qn_ref, kn_ref, u_ref, o_ref, kb_ref, vb_ref, r_ref, acc_ref, flag_ref):
    qi = pl.program_id(2)
    seq = k_ref.shape[0]

    @pl.when(qi == 0)
    def _():
        def body(i, c):
            sl = pl.ds(pl.multiple_of(i * SB_TK, SB_TK), SB_TK)
            k = k_ref[sl, :]
            kb_ref[sl, :] = (k * lax.rsqrt(jnp.mean(k * k, axis=-1, keepdims=True) + RMS_EPS)
                             * kn_ref[...]).astype(BF16)
            vb_ref[sl, :] = v_ref[sl, :].astype(BF16)
            return c

        lax.fori_loop(0, seq // SB_TK, body, 0)

    q = q_ref[...]
    qn = (q * lax.rsqrt(jnp.mean(q * q, axis=-1, keepdims=True) + RMS_EPS) * qn_ref[...]
          * (SB_HEAD_DIM ** -0.5)).astype(BF16)
    r_ref[...] = jnp.zeros_like(r_ref)
    acc_ref[...] = jnp.zeros_like(acc_ref)
    flag_ref[0] = 0
    row = lax.broadcasted_iota(I32, (SB_TQ, SB_TK), 0) + qi * SB_TQ
    col0 = lax.broadcasted_iota(I32, (SB_TQ, SB_TK), 1)

    def body(it, c):
        kb = qi - it

        @pl.when(flag_ref[0] == 0)
        def _():
            ks = pl.ds(pl.multiple_of(kb * SB_TK, SB_TK), SB_TK)
            s = _dot_nt(qn, kb_ref[ks, :])
            valid = (col0 + kb * SB_TK) < row
            ls = jnp.minimum(s, 0.0) - jnp.log1p(jnp.exp(-jnp.abs(s)))
            lk = jnp.where(valid, ls - s, 0.0)
            bw = _dot_hilo(lk, u_ref[...])
            r = r_ref[...]
            p = jnp.where(valid, jnp.exp(ls + bw + r), 0.0)
            acc_ref[...] += _dot(p.astype(BF16), vb_ref[ks, :])
            r_new = r + jnp.sum(lk, axis=-1, keepdims=True)
            r_ref[...] = r_new
            flag_ref[0] = (jnp.max(r_new) < SB_EXIT).astype(I32)

        return c

    lax.fori_loop(0, qi + 1, body, 0)
    o_ref[...] = acc_ref[...].astype(o_ref.dtype)


def sb_attention(proj, q_norm, k_norm, bsz, seq, q_col, k_col, v_col):
    nq = seq // SB_TQ
    r_i = lax.broadcasted_iota(I32, (SB_TK, SB_TK), 0)
    c_i = lax.broadcasted_iota(I32, (SB_TK, SB_TK), 1)
    later = (r_i > c_i).astype(BF16)
    return pl.pallas_call(
        _sb_kernel,
        grid=(bsz, SB_HEADS, nq),
        in_specs=[pl.BlockSpec((SB_TQ, SB_HEAD_DIM), lambda b, h, i: (b * nq + i, q_col + h)),
                  pl.BlockSpec((seq, SB_HEAD_DIM), lambda b, h, i: (b, k_col + h)),
                  pl.BlockSpec((seq, SB_HEAD_DIM), lambda b, h, i: (b, v_col + h)),
                  pl.BlockSpec((1, SB_HEAD_DIM), lambda b, h, i: (0, 0)),
                  pl.BlockSpec((1, SB_HEAD_DIM), lambda b, h, i: (0, 0)),
                  pl.BlockSpec((SB_TK, SB_TK), lambda b, h, i: (0, 0))],
        out_specs=pl.BlockSpec((SB_TQ, SB_HEAD_DIM), lambda b, h, i: (b * nq + i, h)),
        out_shape=jax.ShapeDtypeStruct((bsz * seq, SB_INNER), BF16),
        scratch_shapes=[pltpu.VMEM((seq, SB_HEAD_DIM), BF16),
                        pltpu.VMEM((seq, SB_HEAD_DIM), BF16),
                        pltpu.VMEM((SB_TQ, 1), F32),
                        pltpu.VMEM((SB_TQ, SB_HEAD_DIM), F32),
                        pltpu.SMEM((1,), I32)],
        compiler_params=_cparams(("parallel", "parallel", "arbitrary")),
        name="sb_attention",
    )(proj, proj, proj, q_norm.reshape(1, -1), k_norm.reshape(1, -1), later)


def _conv_silu(ext_ref, w_ref, b, rows):
    acc = b
    for k in range(CONV_K):
        acc = acc + ext_ref[pl.ds(HALO - (CONV_K - 1) + k, rows), :] * w_ref[k:k + 1, :]
    return _silu(acc)


def _ssd_kernel(z_ref, x_ref, bc_ref, dt_ref, cwx_ref, cwbc_ref, cbx_ref, cbbc_ref, dtb_ref, alog_ref, dskip_ref,
                nw_ref, e_ref, o_ref, extx_ref, extbc_ref, h_ref, y_ref):
    c = pl.program_id(1)
    L = SSD_CHUNK

    @pl.when(c == 0)
    def _():
        extx_ref[0:HALO, :] = jnp.zeros((HALO, extx_ref.shape[1]), F32)
        extbc_ref[0:HALO, :] = jnp.zeros((HALO, extbc_ref.shape[1]), F32)
        h_ref[...] = jnp.zeros_like(h_ref)

    @pl.when(c > 0)
    def _():
        extx_ref[0:HALO, :] = extx_ref[L:L + HALO, :]
        extbc_ref[0:HALO, :] = extbc_ref[L:L + HALO, :]

    extx_ref[HALO:HALO + L, :] = x_ref[...]
    extbc_ref[HALO:HALO + L, :] = bc_ref[...]
    xs = _conv_silu(extx_ref, cwx_ref, cbx_ref[...], L)
    bc = _conv_silu(extbc_ref, cwbc_ref, cbbc_ref[...], L)

    dt = _softplus(dt_ref[...] + dtb_ref[...])
    log_a = -jnp.exp(alog_ref[...]) * dt
    r_i = lax.broadcasted_iota(I32, (L, L), 0)
    c_i = lax.broadcasted_iota(I32, (L, L), 1)
    causal = r_i >= c_i
    a_cs = _dot_hilo_lhs(causal.astype(BF16), log_a)
    a_cs_t = a_cs.T
    dt_full = _dot_hilo(dt, e_ref[...])
    acs_full = _dot_hilo(a_cs, e_ref[...])
    acs_last = acs_full[L - 1:L, :]
    xdt = xs * dt_full
    x_end = (xdt * jnp.exp(acs_last - acs_full)).astype(BF16)
    xdt_b = xdt.astype(BF16)

    for g in range(SSD_GROUPS):
        b_g = bc[:, g * SSD_STATE:(g + 1) * SSD_STATE]
        c_g = bc[:, (SSD_GROUPS + g) * SSD_STATE:(SSD_GROUPS + g + 1) * SSD_STATE].astype(BF16)
        cb = _dot_nt(c_g, b_g.astype(BF16))
        gsl = slice(g * SSD_GW, (g + 1) * SSD_GW)
        h_g = h_ref[g]
        y_off = _dot(c_g, h_g.astype(BF16)) * jnp.exp(acs_full[:, gsl])
        for r in range(SSD_HPG):
            h = g * SSD_HPG + r
            diff = a_cs[:, h:h + 1] - a_cs_t[h:h + 1, :]
            scores = (cb * jnp.where(causal, jnp.exp(diff), 0.0)).astype(BF16)
            hsl = slice(h * SSD_HEAD_DIM, (h + 1) * SSD_HEAD_DIM)
            y_ref[:, hsl] = _dot(scores, xdt_b[:, hsl]) + y_off[:, r * SSD_HEAD_DIM:(r + 1) * SSD_HEAD_DIM]
        states = _dot(b_g.T.astype(BF16), x_end[:, gsl])
        h_ref[g] = h_g * jnp.exp(acs_last[:, gsl]) + states

    y = y_ref[...] + xs * dskip_ref[...]
    y = y * _silu(z_ref[...])
    o_ref[...] = (y * lax.rsqrt(jnp.mean(y * y, axis=-1, keepdims=True) + RMS_EPS) * nw_ref[...]).astype(o_ref.dtype)


def ssd_mixer(proj, conv_w, conv_b, dt_bias, a_log, d_skip, ssd_norm, bsz, seq, z_col, x_col, bc_col, dt_col):
    nc = seq // SSD_CHUNK
    L = SSD_CHUNK
    pad = LANES - SSD_HEADS
    dtb = jnp.pad(dt_bias, (0, pad)).reshape(1, LANES)
    alog = jnp.pad(a_log, (0, pad)).reshape(1, LANES)
    dskip = jnp.repeat(d_skip, SSD_HEAD_DIM).reshape(1, SSD_INNER)
    head_of_col = jnp.arange(SSD_INNER, dtype=I32) // SSD_HEAD_DIM
    expand = (jnp.arange(LANES, dtype=I32)[:, None] == head_of_col[None, :]).astype(BF16)
    cwx, cwbc = conv_w[:, :SSD_INNER], conv_w[:, SSD_INNER:]
    cbx, cbbc = conv_b[:SSD_INNER].reshape(1, -1), conv_b[SSD_INNER:].reshape(1, -1)

    def const(b, c):
        return (0, 0)

    return pl.pallas_call(
        _ssd_kernel,
        grid=(bsz, nc),
        in_specs=[pl.BlockSpec((L, SSD_INNER), lambda b, c: (b * nc + c, z_col)),
                  pl.BlockSpec((L, SSD_INNER), lambda b, c: (b * nc + c, x_col)),
                  pl.BlockSpec((L, SSD_BC), lambda b, c: (b * nc + c, bc_col)),
                  pl.BlockSpec((L, LANES), lambda b, c: (b * nc + c, dt_col)),
                  pl.BlockSpec((CONV_K, SSD_INNER), const),
                  pl.BlockSpec((CONV_K, SSD_BC), const),
                  pl.BlockSpec((1, SSD_INNER), const),
                  pl.BlockSpec((1, SSD_BC), const),
                  pl.BlockSpec((1, LANES), const),
                  pl.BlockSpec((1, LANES), const),
                  pl.BlockSpec((1, SSD_INNER), const),
                  pl.BlockSpec((1, SSD_INNER), const),
                  pl.BlockSpec((LANES, SSD_INNER), const)],
        out_specs=pl.BlockSpec((L, SSD_INNER), lambda b, c: (b * nc + c, 0)),
        out_shape=jax.ShapeDtypeStruct((bsz * seq, SSD_INNER), BF16),
        scratch_shapes=[pltpu.VMEM((HALO + L, SSD_INNER), F32),
                        pltpu.VMEM((HALO + L, SSD_BC), F32),
                        pltpu.VMEM((SSD_GROUPS, SSD_STATE, SSD_GW), F32),
                        pltpu.VMEM((L, SSD_INNER), F32)],
        compiler_params=_cparams(("parallel", "arbitrary")),
        name="ssd_mixer",
    )(proj, proj, proj, proj, cwx, cwbc, cbx, cbbc, dtb, alog, dskip, ssd_norm.reshape(1, -1), expand)


def _gdn_kernel(q_ref, k_ref, v_ref, z_ref, ba_ref, cwq_ref, cwk_ref, cwv_ref, alog_ref, dtb_ref, nw_ref, tril_ref,
                o_ref, extq_ref, extk_ref, extv_ref, s_ref, gt_ref):
    kh = pl.program_id(1)
    i = pl.program_id(2)
    R = GDN_RB
    C = GDN_CHUNK

    @pl.when(i == 0)
    def _():
        extq_ref[0:HALO, :] = jnp.zeros((HALO, GDN_DIM), F32)
        extk_ref[0:HALO, :] = jnp.zeros((HALO, GDN_DIM), F32)
        extv_ref[0:HALO, :] = jnp.zeros((HALO, 2 * GDN_DIM), F32)
        s_ref[...] = jnp.zeros_like(s_ref)

    @pl.when(i > 0)
    def _():
        extq_ref[0:HALO, :] = extq_ref[R:R + HALO, :]
        extk_ref[0:HALO, :] = extk_ref[R:R + HALO, :]
        extv_ref[0:HALO, :] = extv_ref[R:R + HALO, :]

    extq_ref[HALO:HALO + R, :] = q_ref[...]
    extk_ref[HALO:HALO + R, :] = k_ref[...]
    extv_ref[HALO:HALO + R, :] = v_ref[...]
    no_bias = jnp.zeros((1, 1), F32)
    q = _conv_silu(extq_ref, cwq_ref, no_bias, R)
    k = _conv_silu(extk_ref, cwk_ref, no_bias, R)
    v = _conv_silu(extv_ref, cwv_ref, no_bias, R)
    qn = q * lax.rsqrt(jnp.sum(q * q, axis=-1, keepdims=True) + RMS_EPS) * (GDN_DIM ** -0.5)
    kn = k * lax.rsqrt(jnp.sum(k * k, axis=-1, keepdims=True) + RMS_EPS)
    qn_b = qn.astype(BF16)
    kn_b = kn.astype(BF16)
    kn_t = kn.T

    ba = ba_ref[...]
    beta_all = jax.nn.sigmoid(ba)
    g_all = -jnp.exp(alog_ref[...]) * _softplus(ba + dtb_ref[...])
    gcs_all = _dot_hilo_lhs(tril_ref[...], g_all)
    gt_ref[...] = gcs_all.T

    qk = _dot_nt(qn_b, kn_b)
    kk = _dot_nt(kn_b, kn_b)
    r_i = lax.broadcasted_iota(I32, (R, R), 0)
    c_i = lax.broadcasted_iota(I32, (R, R), 1)
    same = (r_i // C) == (c_i // C)
    incl = jnp.logical_and(same, r_i >= c_i)
    strict = jnp.logical_and(same, r_i > c_i)
    eye = (r_i == c_i).astype(F32)
    lane = lax.broadcasted_iota(I32, (R, LANES), 1)

    for j in range(2):
        hidx = 2 * kh + j
        beta = jnp.sum(jnp.where(lane == hidx, beta_all, 0.0), axis=-1, keepdims=True)
        gc = jnp.sum(jnp.where(lane == GDN_A_LANE + hidx, gcs_all, 0.0), axis=-1, keepdims=True)
        gr = gt_ref[pl.ds(GDN_A_LANE + hidx, 1), :]
        dec = jnp.where(incl, jnp.exp(gc - gr), 0.0)
        a = jnp.where(strict, kk * beta * dec, 0.0)
        p = eye - a
        m = a
        for _ in range(5):
            m_b = m.astype(BF16)
            m = _dot(m_b, m_b)
            p = p + _dot(p.astype(BF16), m.astype(BF16))
        eg = jnp.exp(gc)
        v_j = v[:, j * GDN_DIM:(j + 1) * GDN_DIM]
        rhs = jnp.concatenate([v_j * beta, kn * (beta * eg)], axis=1).astype(BF16)
        uw = _dot(p.astype(BF16), rhs)
        u = uw[:, :GDN_DIM]
        w = uw[:, GDN_DIM:]
        qkd = jnp.where(incl, qk * dec, 0.0).astype(BF16)
        qd = qn * eg
        outs = []
        for cc in range(R // C):
            rows = slice(cc * C, (cc + 1) * C)
            last = (cc + 1) * C - 1
            s_mat = s_ref[j]
            ws_qs = _dot(jnp.concatenate([w[rows], qd[rows]], axis=0).astype(BF16), s_mat.astype(BF16))
            v_new = u[rows] - ws_qs[:C]
            v_new_b = v_new.astype(BF16)
            outs.append(ws_qs[C:] + _dot(qkd[rows, rows], v_new_b))
            g_last = gc[last:last + 1, :]
            k_end_t = (kn_t[:, rows] * jnp.exp(g_last - gr[:, rows])).astype(BF16)
            s_ref[j] = s_mat * jnp.exp(g_last) + _dot(k_end_t, v_new_b)
        o = jnp.concatenate(outs, axis=0)
        o = o * lax.rsqrt(jnp.mean(o * o, axis=-1, keepdims=True) + RMS_EPS) * nw_ref[...]
        hs = slice(j * GDN_DIM, (j + 1) * GDN_DIM)
        o_ref[:, hs] = (o * _silu(z_ref[:, hs])).astype(o_ref.dtype)


def gdn_mixer(proj, conv_w, dt_bias, a_log, out_norm, bsz, seq, q_col, k_col, v_col, z_col, ba_col):
    R = GDN_RB
    nr = seq // R
    cwq, cwk, cwv = conv_w[:, :GDN_KEY], conv_w[:, GDN_KEY:2 * GDN_KEY], conv_w[:, 2 * GDN_KEY:]
    lane_pad = LANES - GDN_A_LANE - GDN_V_HEADS
    alog = jnp.pad(a_log, (GDN_A_LANE, lane_pad)).reshape(1, LANES)
    dtb = jnp.pad(dt_bias, (GDN_A_LANE, lane_pad)).reshape(1, LANES)
    r_i = lax.broadcasted_iota(I32, (R, R), 0)
    c_i = lax.broadcasted_iota(I32, (R, R), 1)
    tril = jnp.logical_and(r_i // GDN_CHUNK == c_i // GDN_CHUNK, r_i >= c_i).astype(BF16)

    def const(b, h, i):
        return (0, 0)

    return pl.pallas_call(
        _gdn_kernel,
        grid=(bsz, GDN_K_HEADS, nr),
        in_specs=[pl.BlockSpec((R, GDN_DIM), lambda b, h, i: (b * nr + i, q_col + h)),
                  pl.BlockSpec((R, GDN_DIM), lambda b, h, i: (b * nr + i, k_col + h)),
                  pl.BlockSpec((R, 2 * GDN_DIM), lambda b, h, i: (b * nr + i, v_col + h)),
                  pl.BlockSpec((R, 2 * GDN_DIM), lambda b, h, i: (b * nr + i, z_col + h)),
                  pl.BlockSpec((R, LANES), lambda b, h, i: (b * nr + i, ba_col)),
                  pl.BlockSpec((CONV_K, GDN_DIM), lambda b, h, i: (0, h)),
                  pl.BlockSpec((CONV_K, GDN_DIM), lambda b, h, i: (0, h)),
                  pl.BlockSpec((CONV_K, 2 * GDN_DIM), lambda b, h, i: (0, h)),
                  pl.BlockSpec((1, LANES), const),
                  pl.BlockSpec((1, LANES), const),
                  pl.BlockSpec((1, GDN_DIM), const),
                  pl.BlockSpec((R, R), const)],
        out_specs=pl.BlockSpec((R, 2 * GDN_DIM), lambda b, h, i: (b * nr + i, h)),
        out_shape=jax.ShapeDtypeStruct((bsz * seq, GDN_VAL), BF16),
        scratch_shapes=[pltpu.VMEM((HALO + R, GDN_DIM), F32),
                        pltpu.VMEM((HALO + R, GDN_DIM), F32),
                        pltpu.VMEM((HALO + R, 2 * GDN_DIM), F32),
                        pltpu.VMEM((2, GDN_DIM, GDN_DIM), F32),
                        pltpu.VMEM((LANES, R), F32)],
        compiler_params=_cparams(("parallel", "parallel", "arbitrary")),
        name="gdn_mixer",
    )(proj, proj, proj, proj, proj, cwq, cwk, cwv, alog, dtb, out_norm.reshape(1, -1), tril)


ROUTER_TB = 256


def _router_kernel(x_ref, nw_ref, wr_ref, br_ref, hp_ref, ri_ref, rf_ref, cnt_ref, carry_ref):
    tb = ROUTER_TB

    @pl.when(pl.program_id(0) == 0)
    def _():
        carry_ref[...] = jnp.zeros_like(carry_ref)

    x = x_ref[...]
    ms = jnp.mean(x * x, axis=-1, keepdims=True)
    h = x * lax.rsqrt(ms + RMS_EPS) * nw_ref[...]

    lo = h[:, :HALF_D].astype(BF16).astype(F32)
    hi = h[:, HALF_D:].astype(BF16).astype(F32)
    lo_b = lax.shift_right_logical(lax.bitcast_convert_type(lo, U32), jnp.uint32(16))
    hi_b = lax.bitcast_convert_type(hi, U32) & jnp.uint32(0xFFFF0000)
    hp_ref[...] = hi_b | lo_b

    logits = jnp.dot(h, wr_ref[...], preferred_element_type=F32,
                     precision=lax.Precision.HIGHEST) + br_ref[...]
    lane = lax.broadcasted_iota(I32, (tb, LANES), 1)
    vals, idxs = [], []
    cur = logits
    for _ in range(TOP_K):
        m = jnp.max(cur, axis=-1, keepdims=True)
        idx = jnp.min(jnp.where(cur == m, lane, LANES), axis=-1, keepdims=True)
        vals.append(m)
        idxs.append(idx)
        cur = jnp.where(lane == idx, -jnp.inf, cur)

    exps = [jnp.exp(v - vals[0]) for v in vals]
    denom = exps[0] + exps[1] + exps[2] + exps[3]
    gates = [e / denom for e in exps]

    onehot = jnp.zeros((tb, LANES), F32)
    for idx in idxs:
        onehot = onehot + (lane == idx).astype(F32)
    r_i = lax.broadcasted_iota(I32, (tb, tb), 0)
    c_i = lax.broadcasted_iota(I32, (tb, tb), 1)
    tri = (r_i > c_i).astype(BF16)
    carry = carry_ref[0:1, :]
    cum = _dot(tri, onehot.astype(BF16)) + carry
    carry_new = carry + jnp.sum(onehot, axis=0, keepdims=True)
    carry_ref[...] = jnp.broadcast_to(carry_new, carry_ref.shape)
    cnt_ref[...] = jnp.broadcast_to(carry_new, cnt_ref.shape)

    ri = jnp.zeros((tb, LANES), I32)
    rf = jnp.zeros((tb, LANES), F32)
    for k in range(TOP_K):
        rank = jnp.sum(jnp.where(lane == idxs[k], cum, 0.0), axis=-1, keepdims=True).astype(I32)
        ri = jnp.where(lane == k, idxs[k], ri)
        ri = jnp.where(lane == TOP_K + k, rank, ri)
        rf = jnp.where(lane == k, gates[k], rf)
    ri_ref[...] = ri
    rf_ref[...] = rf


def router(x2, nw, w_router, b_router):
    t, d = x2.shape
    tb = ROUTER_TB
    wr = jnp.zeros((d, LANES), F32).at[:, :N_EXPERTS].set(w_router)
    br = jnp.full((1, LANES), -jnp.inf, F32).at[0, :N_EXPERTS].set(b_router)
    return pl.pallas_call(
        _router_kernel,
        grid=(t // tb,),
        in_specs=[pl.BlockSpec((tb, d), lambda i: (i, 0)),
                  pl.BlockSpec((1, d), lambda i: (0, 0)),
                  pl.BlockSpec((d, LANES), lambda i: (0, 0)),
                  pl.BlockSpec((1, LANES), lambda i: (0, 0))],
        out_specs=[pl.BlockSpec((tb, HALF_D), lambda i: (i, 0)),
                   pl.BlockSpec((tb, LANES), lambda i: (i, 0)),
                   pl.BlockSpec((tb, LANES), lambda i: (i, 0)),
                   pl.BlockSpec((SUBLANES, LANES), lambda i: (0, 0))],
        out_shape=[jax.ShapeDtypeStruct((t, HALF_D), U32),
                   jax.ShapeDtypeStruct((t, LANES), I32),
                   jax.ShapeDtypeStruct((t, LANES), F32),
                   jax.ShapeDtypeStruct((SUBLANES, LANES), F32)],
        scratch_shapes=[pltpu.VMEM((SUBLANES, LANES), F32)],
        compiler_params=_cparams(("arbitrary",)),
        name="moe_router",
    )(x2, nw.reshape(1, d), wr, br)


DISPATCH_TB = 256


def _dispatch_kernel(dest_ref, hp_ref, xs_init_ref, xs_ref, sem):
    del xs_init_ref
    tb = DISPATCH_TB

    def body(r, c):
        for k in range(TOP_K):
            d = dest_ref[0, 0, r * TOP_K + k]
            pltpu.make_async_copy(hp_ref.at[pl.ds(r, 1)], xs_ref.at[pl.ds(d, 1)], sem).start()
        return c

    lax.fori_loop(0, tb, body, 0)
    for k in range(TOP_K):
        pltpu.make_async_copy(hp_ref, xs_ref.at[pl.ds(0, tb)], sem).wait()


def dispatch(hp, dest_blocks, n_rows):
    t, w = hp.shape
    tb = DISPATCH_TB
    return pl.pallas_call(
        _dispatch_kernel,
        grid=(t // tb,),
        in_specs=[pl.BlockSpec((1, 1, tb * TOP_K), lambda i: (i, 0, 0), memory_space=pltpu.SMEM),
                  pl.BlockSpec((tb, w), lambda i: (i, 0)),
                  pl.BlockSpec(memory_space=pl.ANY)],
        out_specs=pl.BlockSpec(memory_space=pl.ANY),
        out_shape=jax.ShapeDtypeStruct((n_rows, w), hp.dtype),
        scratch_shapes=[pltpu.SemaphoreType.DMA],
        input_output_aliases={2: 0},
        compiler_params=_cparams(("arbitrary",)),
        name="moe_dispatch",
    )(dest_blocks, hp, jnp.zeros((n_rows, w), hp.dtype))


def _moe_ffn_kernel(te_ref, tb_ref, nr_ref, x_ref, wg_ref, wu_ref, wd_ref, bg_ref, bu_ref, bd_ref, o_ref,
                    xlo_ref, xhi_ref, wgu_ref, wdb_ref):
    j = pl.program_id(0)
    c = pl.program_id(1)
    nrows = nr_ref[j]

    @pl.when(nrows > 0)
    def _():
        @pl.when(c == 0)
        def _():
            for s in range(N_SUB):
                rows = pl.ds(s * SUB_M, SUB_M)
                w = x_ref[rows, :]
                lo = lax.bitcast_convert_type(lax.shift_left(w, jnp.uint32(16)), F32)
                hi = lax.bitcast_convert_type(w & jnp.uint32(0xFFFF0000), F32)
                xlo_ref[rows, :] = lo.astype(BF16)
                xhi_ref[rows, :] = hi.astype(BF16)

        def cast_body(r, carry):
            sl = pl.ds(pl.multiple_of(r * 256, 256), 256)
            wgu_ref[sl, 0:FF_CHUNK] = wg_ref[sl, :].astype(BF16)
            wgu_ref[sl, FF_CHUNK:2 * FF_CHUNK] = wu_ref[sl, :].astype(BF16)
            return carry

        lax.fori_loop(0, D_MODEL // 256, cast_body, 0)
        wdb_ref[...] = wd_ref[...].astype(BF16)

        for s in range(N_SUB):
            rows = pl.ds(s * SUB_M, SUB_M)

            @pl.when(s * SUB_M < nrows)
            def _():
                gu = _dot(xlo_ref[rows, :], wgu_ref[0:HALF_D, :])
                gu = gu + _dot(xhi_ref[rows, :], wgu_ref[HALF_D:D_MODEL, :])
                g = jnp.minimum(gu[:, 0:FF_CHUNK] + bg_ref[...], SWIGLU_LIMIT)
                u = jnp.clip(gu[:, FF_CHUNK:2 * FF_CHUNK] + bu_ref[...], -SWIGLU_LIMIT, SWIGLU_LIMIT)
                act = (u + 1.0) * g * jax.nn.sigmoid(SWIGLU_ALPHA * g)
                contrib = _dot(act.astype(BF16), wdb_ref[...])

                @pl.when(c == 0)
                def _():
                    o_ref[rows, :] = contrib + bd_ref[...]

                @pl.when(c > 0)
                def _():
                    o_ref[rows, :] += contrib

            @pl.when(jnp.logical_and(s * SUB_M >= nrows, c == 0))
            def _():
                o_ref[rows, :] = jnp.zeros((SUB_M, D_MODEL), F32)

    @pl.when(jnp.logical_and(nrows == 0, c == 0))
    def _():
        o_ref[...] = jnp.zeros(o_ref.shape, F32)


def moe_ffn(xs, tile_expert, tile_block, tile_rows, layer, w_gate_up, b_gate_up, w_down, b_down):
    n_rows = xs.shape[0]
    depth = w_gate_up.shape[0]
    bgu = b_gate_up.reshape(depth, N_EXPERTS, 1, 2 * EXPERT_FF)
    bd = b_down.reshape(depth, N_EXPERTS, 1, D_MODEL)
    grid_spec = pltpu.PrefetchScalarGridSpec(
        num_scalar_prefetch=3,
        grid=(n_rows // TILE_M, N_FF_CHUNKS),
        in_specs=[
            pl.BlockSpec((TILE_M, HALF_D), lambda j, c, te, tb, nr: (tb[j], 0)),
            pl.BlockSpec((None, None, D_MODEL, FF_CHUNK), lambda j, c, te, tb, nr: (layer, te[j], 0, c)),
            pl.BlockSpec((None, None, D_MODEL, FF_CHUNK), lambda j, c, te, tb, nr: (layer, te[j], 0, N_FF_CHUNKS + c)),
            pl.BlockSpec((None, None, FF_CHUNK, D_MODEL), lambda j, c, te, tb, nr: (layer, te[j], c, 0)),
            pl.BlockSpec((None, None, 1, FF_CHUNK), lambda j, c, te, tb, nr: (layer, te[j], 0, c)),
            pl.BlockSpec((None, None, 1, FF_CHUNK), lambda j, c, te, tb, nr: (layer, te[j], 0, N_FF_CHUNKS + c)),
            pl.BlockSpec((None, None, 1, D_MODEL), lambda j, c, te, tb, nr: (layer, te[j], 0, 0)),
        ],
        out_specs=pl.BlockSpec((TILE_M, D_MODEL), lambda j, c, te, tb, nr: (j, 0)),
        scratch_shapes=[pltpu.VMEM((TILE_M, HALF_D), BF16),
                        pltpu.VMEM((TILE_M, HALF_D), BF16),
                        pltpu.VMEM((D_MODEL, 2 * FF_CHUNK), BF16),
                        pltpu.VMEM((FF_CHUNK, D_MODEL), BF16)],
    )
    return pl.pallas_call(
        _moe_ffn_kernel,
        grid_spec=grid_spec,
        out_shape=jax.ShapeDtypeStruct((n_rows, D_MODEL), F32),
        compiler_params=_cparams(("arbitrary", "arbitrary")),
        name="moe_ffn",
    )(tile_expert, tile_block, tile_rows, xs, w_gate_up, w_gate_up, w_down, bgu, bgu, bd)


COMBINE_TB = 256


def _combine_kernel(dest_ref, x_ref, g_ref, ys_ref, o_ref, buf_ref, sem):
    tb = COMBINE_TB

    def body(r, c):
        for k in range(TOP_K):
            d = dest_ref[0, 0, r * TOP_K + k]
            pltpu.make_async_copy(ys_ref.at[pl.ds(d, 1)], buf_ref.at[k, pl.ds(r, 1)], sem).start()
        return c

    lax.fori_loop(0, tb, body, 0)
    for k in range(TOP_K):
        pltpu.make_async_copy(ys_ref.at[pl.ds(0, tb)], buf_ref.at[k], sem).wait()
    acc = x_ref[...]
    g = g_ref[...]
    for k in range(TOP_K):
        acc = acc + g[:, k:k + 1] * buf_ref[k]
    o_ref[...] = acc


def combine(x2, gates, ys, dest_blocks):
    t, d = x2.shape
    tb = COMBINE_TB
    return pl.pallas_call(
        _combine_kernel,
        grid=(t // tb,),
        in_specs=[pl.BlockSpec((1, 1, tb * TOP_K), lambda i: (i, 0, 0), memory_space=pltpu.SMEM),
                  pl.BlockSpec((tb, d), lambda i: (i, 0)),
                  pl.BlockSpec((tb, LANES), lambda i: (i, 0)),
                  pl.BlockSpec(memory_space=pl.ANY)],
        out_specs=pl.BlockSpec((tb, d), lambda i: (i, 0)),
        out_shape=jax.ShapeDtypeStruct((t, d), F32),
        scratch_shapes=[pltpu.VMEM((TOP_K, tb, d), F32), pltpu.SemaphoreType.DMA],
        compiler_params=_cparams(("arbitrary",)),
        name="moe_combine",
    )(dest_blocks, x2, gates, ys)


def moe_layer(x2, nw, w_router, b_router, layer, w_gate_up, b_gate_up, w_down, b_down):
    t = x2.shape[0]
    max_tiles = (t * TOP_K) // TILE_M + N_EXPERTS
    hp, ri, rf, cnt = router(x2, nw, w_router, b_router)
    idx = ri[:, 0:TOP_K]
    rank = ri[:, TOP_K:2 * TOP_K]
    counts = cnt[0, :N_EXPERTS].astype(I32)

    tiles_e = (counts + TILE_M - 1) // TILE_M
    tile_end = jnp.cumsum(tiles_e)
    tile_start = tile_end - tiles_e
    n_tiles = tile_end[-1]
    dest = (tile_start * TILE_M)[idx] + rank
    j = jnp.arange(max_tiles, dtype=I32)
    jc = jnp.minimum(j, n_tiles - 1)
    te = jnp.sum((jc[:, None] >= tile_end[None, :]).astype(I32), axis=1)
    rows_left = counts[te] - (jc - tile_start[te]) * TILE_M
    tile_rows = jnp.where(j < n_tiles, jnp.clip(rows_left, 0, TILE_M), 0).astype(I32)
    dest_blocks = dest.reshape(t // DISPATCH_TB, 1, DISPATCH_TB * TOP_K).astype(I32)

    xs = dispatch(hp, dest_blocks, max_tiles * TILE_M)
    ys = moe_ffn(xs, te.astype(I32), jc.astype(I32), tile_rows, layer, w_gate_up, b_gate_up, w_down, b_down)
    return combine(x2, rf, ys, dest_blocks)


def even_mixer(x2, bsz, seq, nw, w_in, conv_w, conv_b, dt_bias, a_log, d_skip, ssd_norm, q_norm, k_norm, w_out):
    d = x2.shape[1]
    o2 = SSD_INNER + SSD_INNER + SSD_BC
    o3 = o2 + SSD_HEADS
    w_cat = jnp.concatenate(
        [w_in[:, :o2], w_in[:, o3:], w_in[:, o2:o3], jnp.zeros((d, PROJ_TN - SSD_HEADS), F32)], axis=1).astype(BF16)
    proj = norm_matmul(x2, nw, w_cat)
    y_ssd = ssd_mixer(proj, conv_w, conv_b, dt_bias, a_log, d_skip, ssd_norm, bsz, seq,
                      z_col=0, x_col=1, bc_col=(2 * SSD_INNER) // SSD_BC, dt_col=(o2 + 3 * SB_INNER) // LANES)
    y_sb = sb_attention(proj, q_norm, k_norm, bsz, seq, q_col=o2 // LANES, k_col=(o2 + SB_INNER) // LANES,
                        v_col=(o2 + 2 * SB_INNER) // LANES)
    w_out_bf = w_out.astype(BF16)
    return matmul_residual([y_ssd, y_sb], [w_out_bf[:SSD_INNER], w_out_bf[SSD_INNER:]], x2)


def odd_mixer(x2, bsz, seq, nw, w_in, conv_w, dt_bias, a_log, out_norm, w_out):
    d = x2.shape[1]
    n_in = w_in.shape[1]
    n_pad = -n_in % PROJ_TN
    w_cat = jnp.concatenate([w_in, jnp.zeros((d, n_pad), F32)], axis=1).astype(BF16)
    proj = norm_matmul(x2, nw, w_cat)
    o = gdn_mixer(proj, conv_w, dt_bias, a_log, out_norm, bsz, seq,
                  q_col=0, k_col=GDN_KEY // LANES, v_col=(2 * GDN_KEY) // (2 * GDN_DIM),
                  z_col=(2 * GDN_KEY + GDN_VAL) // (2 * GDN_DIM), ba_col=(2 * GDN_KEY + 2 * GDN_VAL) // LANES)
    return matmul_residual([o], [w_out.astype(BF16)], x2)


def kernel(x, mix_norm, ffn_norm, ev_w_in, ev_conv_w, ev_conv_b, ev_dt_bias, ev_a_log, ev_d_skip, ev_ssd_norm, ev_q_norm, ev_k_norm, ev_w_out, od_w_in, od_conv_w, od_dt_bias, od_a_log, od_out_norm, od_w_out, moe_w_router, moe_b_router, moe_w_gate_up, moe_b_gate_up, moe_w_down, moe_b_down):
    bsz, seq, d = x.shape
    depth = mix_norm.shape[0]
    x2 = x.reshape(bsz * seq, d)
    for layer in range(depth):
        i = layer // 2
        if layer % 2 == 0:
            x2 = even_mixer(x2, bsz, seq, mix_norm[layer], ev_w_in[i], ev_conv_w[i], ev_conv_b[i], ev_dt_bias[i],
                            ev_a_log[i], ev_d_skip[i], ev_ssd_norm[i], ev_q_norm[i], ev_k_norm[i], ev_w_out[i])
        else:
            x2 = odd_mixer(x2, bsz, seq, mix_norm[layer], od_w_in[i], od_conv_w[i], od_dt_bias[i], od_a_log[i],
                           od_out_norm[i], od_w_out[i])
        x2 = moe_layer(x2, ffn_norm[layer], moe_w_router[layer], moe_b_router[layer], layer,
                       moe_w_gate_up, moe_b_gate_up, moe_w_down, moe_b_down)
    return x2.reshape(bsz, seq, d)
```

```python
import jax
import jax.numpy as jnp
from jax import lax
from jax.experimental import pallas as pl
from jax.experimental.pallas import tpu as pltpu

F32 = jnp.float32
BF16 = jnp.bfloat16
I32 = jnp.int32
U32 = jnp.uint32

D_MODEL = 2048
RMS_EPS = 1e-6
CONV_K = 4

SSD_HEADS = 32
SSD_HEAD_DIM = 64
SSD_INNER = 2048
SSD_GROUPS = 4
SSD_HPG = 8
SSD_STATE = 128
SSD_CHUNK = 128
SSD_GW = SSD_HPG * SSD_HEAD_DIM
SSD_BC = 2 * SSD_GROUPS * SSD_STATE

SB_HEADS = 16
SB_HEAD_DIM = 128
SB_INNER = 2048
SB_TQ = 256
SB_TK = 256
SB_EXIT = -104.0

GDN_K_HEADS = 16
GDN_V_HEADS = 32
GDN_DIM = 128
GDN_KEY = GDN_K_HEADS * GDN_DIM
GDN_VAL = GDN_V_HEADS * GDN_DIM
GDN_CHUNK = 64
GDN_RB = 256
GDN_A_LANE = GDN_V_HEADS
GDN_KH_STEP = 2

N_EXPERTS = 32
TOP_K = 4
EXPERT_FF = 1024
SWIGLU_LIMIT = 7.0
SWIGLU_ALPHA = 1.702

LANES = 128
SUBLANES = 8
VMEM_LIMIT = 56 * 1024 * 1024

TILE_M = 1024
SUB_M = 256
N_SUB = TILE_M // SUB_M
FF_CHUNK = 256
N_FF_CHUNKS = EXPERT_FF // FF_CHUNK
D_CHUNK = 512
N_D_CHUNKS = D_MODEL // D_CHUNK
HALF_D = D_MODEL // 2
PROJ_TN = 512

HALO = SUBLANES


def _cparams(sem):
    return pltpu.CompilerParams(dimension_semantics=sem, vmem_limit_bytes=VMEM_LIMIT)


def _dot(a, b):
    return jnp.dot(a, b, preferred_element_type=F32)


def _dot_nt(a, b):
    return lax.dot_general(a, b, (((1,), (1,)), ((), ())), preferred_element_type=F32)


def _split_bf16(x):
    hi = x.astype(BF16)
    lo = (x - hi.astype(F32)).astype(BF16)
    return hi, lo


def _dot_hilo(x, m_bf):
    hi, lo = _split_bf16(x)
    return _dot(hi, m_bf) + _dot(lo, m_bf)


def _dot_hilo_lhs(m_bf, x):
    hi, lo = _split_bf16(x)
    return _dot(m_bf, hi) + _dot(m_bf, lo)


def _softplus(x):
    return jnp.maximum(x, 0.0) + jnp.log1p(jnp.exp(-jnp.abs(x)))


def _silu(x):
    return x * jax.nn.sigmoid(x)


def _norm_matmul_kernel(x_ref, nw_ref, w_ref, o_ref, h_ref):
    @pl.when(pl.program_id(1) == 0)
    def _():
        rows = 128

        def body(r, c):
            sl = pl.ds(pl.multiple_of(r * rows, rows), rows)
            x = x_ref[sl, :]
            ms = jnp.mean(x * x, axis=-1, keepdims=True)
            h_ref[sl, :] = (x * lax.rsqrt(ms + RMS_EPS) * nw_ref[...]).astype(BF16)
            return c

        lax.fori_loop(0, x_ref.shape[0] // rows, body, 0)

    o_ref[...] = _dot(h_ref[...], w_ref[...])


def norm_matmul(x2, nw, w_bf, tm=1024, tn=PROJ_TN):
    t, d = x2.shape
    n = w_bf.shape[1]
    return pl.pallas_call(
        _norm_matmul_kernel,
        grid=(t // tm, n // tn),
        in_specs=[pl.BlockSpec((tm, d), lambda i, j: (i, 0)),
                  pl.BlockSpec((1, d), lambda i, j: (0, 0)),
                  pl.BlockSpec((d, tn), lambda i, j: (0, j))],
        out_specs=pl.BlockSpec((tm, tn), lambda i, j: (i, j)),
        out_shape=jax.ShapeDtypeStruct((t, n), F32),
        scratch_shapes=[pltpu.VMEM((tm, d), BF16)],
        compiler_params=_cparams(("parallel", "arbitrary")),
        name="norm_matmul",
    )(x2, nw.reshape(1, d), w_bf)


def _matmul_residual_kernel(*refs):
    n_in = (len(refs) - 2) // 2
    x_ref, o_ref = refs[2 * n_in], refs[2 * n_in + 1]
    acc = x_ref[...]
    for i in range(n_in):
        acc = acc + _dot(refs[i][...], refs[n_in + i][...])
    o_ref[...] = acc


def matmul_residual(ys, ws, x2, tm=1024, tn=512):
    t, n = x2.shape
    in_specs = [pl.BlockSpec((tm, y.shape[1]), lambda i, j: (i, 0)) for y in ys]
    in_specs += [pl.BlockSpec((w.shape[0], tn), lambda i, j: (0, j)) for w in ws]
    in_specs += [pl.BlockSpec((tm, tn), lambda i, j: (i, j))]
    return pl.pallas_call(
        _matmul_residual_kernel,
        grid=(t // tm, n // tn),
        in_specs=in_specs,
        out_specs=pl.BlockSpec((tm, tn), lambda i, j: (i, j)),
        out_shape=jax.ShapeDtypeStruct((t, n), F32),
        compiler_params=_cparams(("parallel", "parallel")),
        name="matmul_residual",
    )(*ys, *ws, x2)


def _sb_kernel(q_ref, k_ref, v_ref, qn_ref, kn_ref, u_ref, o_ref, kb_ref, vb_ref, r_ref, acc_ref, flag_ref):
    qi = pl.program_id(2)
    seq = k_ref.shape[0]

    @pl.when(qi == 0)
    def _():
        def body(i, c):
            sl = pl.ds(pl.multiple_of(i * SB_TK, SB_TK), SB_TK)
            k = k_ref[sl, :]
            kb_ref[sl, :] = (k * lax.rsqrt(jnp.mean(k * k, axis=-1, keepdims=True) + RMS_EPS)
                             * kn_ref[...]).astype(BF16)
            vb_ref[sl, :] = v_ref[sl, :].astype(BF16)
            return c

        lax.fori_loop(0, seq // SB_TK, body, 0)

    q = q_ref[...]
    qn = (q * lax.rsqrt(jnp.mean(q * q, axis=-1, keepdims=True) + RMS_EPS) * qn_ref[...]
          * (SB_HEAD_DIM ** -0.5)).astype(BF16)
    r_ref[...] = jnp.zeros_like(r_ref)
    acc_ref[...] = jnp.zeros_like(acc_ref)
    flag_ref[0] = 0
    row = lax.broadcasted_iota(I32, (SB_TQ, SB_TK), 0) + qi * SB_TQ
    col0 = lax.broadcasted_iota(I32, (SB_TQ, SB_TK), 1)

    def body(it, c):
        kb = qi - it

        @pl.when(flag_ref[0] == 0)
        def _():
            ks = pl.ds(pl.multiple_of(kb * SB_TK, SB_TK), SB_TK)
            s = _dot_nt(qn, kb_ref[ks, :])
            valid = (col0 + kb * SB_TK) < row
            ls = jnp.minimum(s, 0.0) - jnp.log1p(jnp.exp(-jnp.abs(s)))
            lk = jnp.where(valid, ls - s, 0.0)
            bw = _dot_hilo(lk, u_ref[...])
            r = r_ref[...]
            p = jnp.where(valid, jnp.exp(ls + bw + r), 0.0)
            acc_ref[...] += _dot(p.astype(BF16), vb_ref[ks, :])
            r_new = r + jnp.sum(lk, axis=-1, keepdims=True)
            r_ref[...] = r_new
            flag_ref[0] = (jnp.max(r_new) < SB_EXIT).astype(I32)

        return c

    lax.fori_loop(0, qi + 1, body, 0)
    o_ref[...] = acc_ref[...].astype(o_ref.dtype)


def sb_attention(proj, q_norm, k_norm, bsz, seq, q_col, k_col, v_col):
    nq = seq // SB_TQ
    r_i = lax.broadcasted_iota(I32, (SB_TK, SB_TK), 0)
    c_i = lax.broadcasted_iota(I32, (SB_TK, SB_TK), 1)
    later = (r_i > c_i).astype(BF16)
    return pl.pallas_call(
        _sb_kernel,
        grid=(bsz, SB_HEADS, nq),
        in_specs=[pl.BlockSpec((SB_TQ, SB_HEAD_DIM), lambda b, h, i: (b * nq + i, q_col + h)),
                  pl.BlockSpec((seq, SB_HEAD_DIM), lambda b, h, i: (b, k_col + h)),
                  pl.BlockSpec((seq, SB_HEAD_DIM), lambda b, h, i: (b, v_col + h)),
                  pl.BlockSpec((1, SB_HEAD_DIM), lambda b, h, i: (0, 0)),
                  pl.BlockSpec((1, SB_HEAD_DIM), lambda b, h, i: (0, 0)),
                  pl.BlockSpec((SB_TK, SB_TK), lambda b, h, i: (0, 0))],
        out_specs=pl.BlockSpec((SB_TQ, SB_HEAD_DIM), lambda b, h, i: (b * nq + i, h)),
        out_shape=jax.ShapeDtypeStruct((bsz * seq, SB_INNER), BF16),
        scratch_shapes=[pltpu.VMEM((seq, SB_HEAD_DIM), BF16),
                        pltpu.VMEM((seq, SB_HEAD_DIM), BF16),
                        pltpu.VMEM((SB_TQ, 1), F32),
                        pltpu.VMEM((SB_TQ, SB_HEAD_DIM), F32),
                        pltpu.SMEM((1,), I32)],
        compiler_params=_cparams(("parallel", "parallel", "arbitrary")),
        name="sb_attention",
    )(proj, proj, proj, q_norm.reshape(1, -1), k_norm.reshape(1, -1), later)


def _conv_silu(ext_ref, w_ref, b, rows):
    acc = b
    for k in range(CONV_K):
        acc = acc + ext_ref[pl.ds(HALO - (CONV_K - 1) + k, rows), :] * w_ref[k:k + 1, :]
    return _silu(acc)


def _ssd_kernel(z_ref, x_ref, bc_ref, dt_ref, cwx_ref, cwbc_ref, cbx_ref, cbbc_ref, dtb_ref, alog_ref, dskip_ref,
                nw_ref, e_ref, o_ref, extx_ref, extbc_ref, h_ref, y_ref):
    c = pl.program_id(1)
    L = SSD_CHUNK

    @pl.when(c == 0)
    def _():
        extx_ref[0:HALO, :] = jnp.zeros((HALO, extx_ref.shape[1]), F32)
        extbc_ref[0:HALO, :] = jnp.zeros((HALO, extbc_ref.shape[1]), F32)
        h_ref[...] = jnp.zeros_like(h_ref)

    @pl.when(c > 0)
    def _():
        extx_ref[0:HALO, :] = extx_ref[L:L + HALO, :]
        extbc_ref[0:HALO, :] = extbc_ref[L:L + HALO, :]

    extx_ref[HALO:HALO + L, :] = x_ref[...]
    extbc_ref[HALO:HALO + L, :] = bc_ref[...]
    xs = _conv_silu(extx_ref, cwx_ref, cbx_ref[...], L)
    bc = _conv_silu(extbc_ref, cwbc_ref, cbbc_ref[...], L)

    dt = _softplus(dt_ref[...] + dtb_ref[...])
    log_a = -jnp.exp(alog_ref[...]) * dt
    r_i = lax.broadcasted_iota(I32, (L, L), 0)
    c_i = lax.broadcasted_iota(I32, (L, L), 1)
    causal = r_i >= c_i
    a_cs = _dot_hilo_lhs(causal.astype(BF16), log_a)
    a_cs_t = a_cs.T
    dt_full = _dot_hilo(dt, e_ref[...])
    acs_full = _dot_hilo(a_cs, e_ref[...])
    acs_last = acs_full[L - 1:L, :]
    xdt = xs * dt_full
    x_end = (xdt * jnp.exp(acs_last - acs_full)).astype(BF16)
    xdt_b = xdt.astype(BF16)

    for g in range(SSD_GROUPS):
        b_g = bc[:, g * SSD_STATE:(g + 1) * SSD_STATE]
        c_g = bc[:, (SSD_GROUPS + g) * SSD_STATE:(SSD_GROUPS + g + 1) * SSD_STATE].astype(BF16)
        cb = _dot_nt(c_g, b_g.astype(BF16))
        gsl = slice(g * SSD_GW, (g + 1) * SSD_GW)
        h_g = h_ref[g]
        y_off = _dot(c_g, h_g.astype(BF16)) * jnp.exp(acs_full[:, gsl])
        for r in range(SSD_HPG):
            h = g * SSD_HPG + r
            diff = a_cs[:, h:h + 1] - a_cs_t[h:h + 1, :]
            scores = (cb * jnp.where(causal, jnp.exp(diff), 0.0)).astype(BF16)
            hsl = slice(h * SSD_HEAD_DIM, (h + 1) * SSD_HEAD_DIM)
            y_ref[:, hsl] = _dot(scores, xdt_b[:, hsl]) + y_off[:, r * SSD_HEAD_DIM:(r + 1) * SSD_HEAD_DIM]
        states = _dot(b_g.T.astype(BF16), x_end[:, gsl])
        h_ref[g] = h_g * jnp.exp(acs_last[:, gsl]) + states

    y = y_ref[...] + xs * dskip_ref[...]
    y = y * _silu(z_ref[...])
    o_ref[...] = (y * lax.rsqrt(jnp.mean(y * y, axis=-1, keepdims=True) + RMS_EPS) * nw_ref[...]).astype(o_ref.dtype)


def ssd_mixer(proj, conv_w, conv_b, dt_bias, a_log, d_skip, ssd_norm, bsz, seq, z_col, x_col, bc_col, dt_col):
    nc = seq // SSD_CHUNK
    L = SSD_CHUNK
    pad = LANES - SSD_HEADS
    dtb = jnp.pad(dt_bias, (0, pad)).reshape(1, LANES)
    alog = jnp.pad(a_log, (0, pad)).reshape(1, LANES)
    dskip = jnp.repeat(d_skip, SSD_HEAD_DIM).reshape(1, SSD_INNER)
    head_of_col = jnp.arange(SSD_INNER, dtype=I32) // SSD_HEAD_DIM
    expand = (jnp.arange(LANES, dtype=I32)[:, None] == head_of_col[None, :]).astype(BF16)
    cwx, cwbc = conv_w[:, :SSD_INNER], conv_w[:, SSD_INNER:]
    cbx, cbbc = conv_b[:SSD_INNER].reshape(1, -1), conv_b[SSD_INNER:].reshape(1, -1)

    def const(b, c):
        return (0, 0)

    return pl.pallas_call(
        _ssd_kernel,
        grid=(bsz, nc),
        in_specs=[pl.BlockSpec((L, SSD_INNER), lambda b, c: (b * nc + c, z_col)),
                  pl.BlockSpec((L, SSD_INNER), lambda b, c: (b * nc + c, x_col)),
                  pl.BlockSpec((L, SSD_BC), lambda b, c: (b * nc + c, bc_col)),
                  pl.BlockSpec((L, LANES), lambda b, c: (b * nc + c, dt_col)),
                  pl.BlockSpec((CONV_K, SSD_INNER), const),
                  pl.BlockSpec((CONV_K, SSD_BC), const),
                  pl.BlockSpec((1, SSD_INNER), const),
                  pl.BlockSpec((1, SSD_BC), const),
                  pl.BlockSpec((1, LANES), const),
                  pl.BlockSpec((1, LANES), const),
                  pl.BlockSpec((1, SSD_INNER), const),
                  pl.BlockSpec((1, SSD_INNER), const),
                  pl.BlockSpec((LANES, SSD_INNER), const)],
        out_specs=pl.BlockSpec((L, SSD_INNER), lambda b, c: (b * nc + c, 0)),
        out_shape=jax.ShapeDtypeStruct((bsz * seq, SSD_INNER), BF16),
        scratch_shapes=[pltpu.VMEM((HALO + L, SSD_INNER), F32),
                        pltpu.VMEM((HALO + L, SSD_BC), F32),
                        pltpu.VMEM((SSD_GROUPS, SSD_STATE, SSD_GW), F32),
                        pltpu.VMEM((L, SSD_INNER), F32)],
        compiler_params=_cparams(("parallel", "arbitrary")),
        name="ssd_mixer",
    )(proj, proj, proj, proj, cwx, cwbc, cbx, cbbc, dtb, alog, dskip, ssd_norm.reshape(1, -1), expand)


def _gdn_kernel(q_ref, k_ref, v_ref, z_ref, ba_ref, cwq_ref, cwk_ref, cwv_ref, alog_ref, dtb_ref, nw_ref, tril_ref,
                o_ref, extq_ref, extk_ref, extv_ref, s_ref, gt_ref):
    hp = pl.program_id(1)
    i = pl.program_id(2)
    R = GDN_RB
    C = GDN_CHUNK
    D = GDN_DIM

    @pl.when(i == 0)
    def _():
        extq_ref[0:HALO, :] = jnp.zeros((HALO, extq_ref.shape[1]), F32)
        extk_ref[0:HALO, :] = jnp.zeros((HALO, extk_ref.shape[1]), F32)
        extv_ref[0:HALO, :] = jnp.zeros((HALO, extv_ref.shape[1]), F32)
        s_ref[...] = jnp.zeros_like(s_ref)

    @pl.when(i > 0)
    def _():
        extq_ref[0:HALO, :] = extq_ref[R:R + HALO, :]
        extk_ref[0:HALO, :] = extk_ref[R:R + HALO, :]
        extv_ref[0:HALO, :] = extv_ref[R:R + HALO, :]

    extq_ref[HALO:HALO + R, :] = q_ref[...]
    extk_ref[HALO:HALO + R, :] = k_ref[...]
    extv_ref[HALO:HALO + R, :] = v_ref[...]
    no_bias = jnp.zeros((1, 1), F32)
    q_all = _conv_silu(extq_ref, cwq_ref, no_bias, R)
    k_all = _conv_silu(extk_ref, cwk_ref, no_bias, R)
    v_all = _conv_silu(extv_ref, cwv_ref, no_bias, R)

    ba = ba_ref[...]
    beta_all = jax.nn.sigmoid(ba)
    g_all = -jnp.exp(alog_ref[...]) * _softplus(ba + dtb_ref[...])
    gcs_all = _dot_hilo_lhs(tril_ref[...], g_all)
    gt_ref[...] = gcs_all.T

    r_i = lax.broadcasted_iota(I32, (R, R), 0)
    c_i = lax.broadcasted_iota(I32, (R, R), 1)
    same = (r_i // C) == (c_i // C)
    incl = jnp.logical_and(same, r_i >= c_i)
    strict = jnp.logical_and(same, r_i > c_i)
    eye = (r_i == c_i).astype(F32)
    lane = lax.broadcasted_iota(I32, (R, LANES), 1)

    for kh in range(GDN_KH_STEP):
        q = q_all[:, kh * D:(kh + 1) * D]
        k = k_all[:, kh * D:(kh + 1) * D]
        qn = q * lax.rsqrt(jnp.sum(q * q, axis=-1, keepdims=True) + RMS_EPS) * (D ** -0.5)
        kn = k * lax.rsqrt(jnp.sum(k * k, axis=-1, keepdims=True) + RMS_EPS)
        qn_b = qn.astype(BF16)
        kn_b = kn.astype(BF16)
        kn_t = kn.T
        qk = _dot_nt(qn_b, kn_b)
        kk = _dot_nt(kn_b, kn_b)
        for j in range(2):
            vh = 2 * kh + j
            hidx = 2 * GDN_KH_STEP * hp + vh
            beta = jnp.sum(jnp.where(lane == hidx, beta_all, 0.0), axis=-1, keepdims=True)
            gc = jnp.sum(jnp.where(lane == GDN_A_LANE + hidx, gcs_all, 0.0), axis=-1, keepdims=True)
            gr = gt_ref[pl.ds(GDN_A_LANE + hidx, 1), :]
            dec = jnp.where(incl, jnp.exp(gc - gr), 0.0)
            a = jnp.where(strict, kk * beta * dec, 0.0)
            p = eye - a
            m = a
            for _ in range(5):
                m_b = m.astype(BF16)
                m = _dot(m_b, m_b)
                p = p + _dot(p.astype(BF16), m.astype(BF16))
            eg = jnp.exp(gc)
            v_j = v_all[:, vh * D:(vh + 1) * D]
            rhs = jnp.concatenate([kn * (beta * eg), v_j * beta], axis=1).astype(BF16)
            wu = _dot(p.astype(BF16), rhs)
            w = wu[:, :D]
            u = wu[:, D:]
            wu_b = wu.astype(BF16)
            qkd = jnp.where(incl, qk * dec, 0.0).astype(BF16)
            qd = qn * eg
            kw_ku, g_last = [], []
            for cc in range(R // C):
                rows = slice(cc * C, (cc + 1) * C)
                last = (cc + 1) * C - 1
                gl = gc[last:last + 1, :]
                k_end_t = (kn_t[:, rows] * jnp.exp(gl - gr[:, rows])).astype(BF16)
                g_last.append(gl)
                kw_ku.append(_dot(k_end_t, wu_b[rows]))
            outs = []
            for cc in range(R // C):
                rows = slice(cc * C, (cc + 1) * C)
                s_mat = s_ref[vh]
                s_b = s_mat.astype(BF16)
                ws_qs = _dot(jnp.concatenate([w[rows], qd[rows]], axis=0).astype(BF16), s_b)
                v_new = u[rows] - ws_qs[:C]
                outs.append(ws_qs[C:] + _dot(qkd[rows, rows], v_new.astype(BF16)))
                s_ref[vh] = (s_mat * jnp.exp(g_last[cc]) - _dot(kw_ku[cc][:, :D].astype(BF16), s_b)
                             + kw_ku[cc][:, D:])
            o = jnp.concatenate(outs, axis=0)
            o = o * lax.rsqrt(jnp.mean(o * o, axis=-1, keepdims=True) + RMS_EPS) * nw_ref[...]
            hs = slice(vh * D, (vh + 1) * D)
            o_ref[:, hs] = (o * _silu(z_ref[:, hs])).astype(o_ref.dtype)


def gdn_mixer(proj, conv_w, dt_bias, a_log, out_norm, bsz, seq, q_col, k_col, v_col, z_col):
    R = GDN_RB
    nr = seq // R
    qw = GDN_KH_STEP * GDN_DIM
    vw = 2 * qw
    ba_col = (2 * GDN_KEY + 2 * GDN_VAL) // LANES
    cwq, cwk, cwv = conv_w[:, :GDN_KEY], conv_w[:, GDN_KEY:2 * GDN_KEY], conv_w[:, 2 * GDN_KEY:]
    lane_pad = LANES - GDN_A_LANE - GDN_V_HEADS
    alog = jnp.pad(a_log, (GDN_A_LANE, lane_pad)).reshape(1, LANES)
    dtb = jnp.pad(dt_bias, (GDN_A_LANE, lane_pad)).reshape(1, LANES)
    r_i = lax.broadcasted_iota(I32, (R, R), 0)
    c_i = lax.broadcasted_iota(I32, (R, R), 1)
    tril = jnp.logical_and(r_i // GDN_CHUNK == c_i // GDN_CHUNK, r_i >= c_i).astype(BF16)

    def const(b, h, i):
        return (0, 0)

    return pl.pallas_call(
        _gdn_kernel,
        grid=(bsz, GDN_K_HEADS // GDN_KH_STEP, nr),
        in_specs=[pl.BlockSpec((R, qw), lambda b, h, i: (b * nr + i, q_col + h)),
                  pl.BlockSpec((R, qw), lambda b, h, i: (b * nr + i, k_col + h)),
                  pl.BlockSpec((R, vw), lambda b, h, i: (b * nr + i, v_col + h)),
                  pl.BlockSpec((R, vw), lambda b, h, i: (b * nr + i, z_col + h)),
                  pl.BlockSpec((R, LANES), lambda b, h, i: (b * nr + i, ba_col)),
                  pl.BlockSpec((CONV_K, qw), lambda b, h, i: (0, h)),
                  pl.BlockSpec((CONV_K, qw), lambda b, h, i: (0, h)),
                  pl.BlockSpec((CONV_K, vw), lambda b, h, i: (0, h)),
                  pl.BlockSpec((1, LANES), const),
                  pl.BlockSpec((1, LANES), const),
                  pl.BlockSpec((1, GDN_DIM), const),
                  pl.BlockSpec((R, R), const)],
        out_specs=pl.BlockSpec((R, vw), lambda b, h, i: (b * nr + i, h)),
        out_shape=jax.ShapeDtypeStruct((bsz * seq, GDN_VAL), BF16),
        scratch_shapes=[pltpu.VMEM((HALO + R, qw), F32),
                        pltpu.VMEM((HALO + R, qw), F32),
                        pltpu.VMEM((HALO + R, vw), F32),
                        pltpu.VMEM((2 * GDN_KH_STEP, GDN_DIM, GDN_DIM), F32),
                        pltpu.VMEM((LANES, R), F32)],
        compiler_params=_cparams(("parallel", "parallel", "arbitrary")),
        name="gdn_mixer",
    )(proj, proj, proj, proj, proj, cwq, cwk, cwv, alog, dtb, out_norm.reshape(1, -1), tril)


ROUTER_TB = 256


def _router_kernel(x_ref, nw_ref, wr_ref, br_ref, hp_ref, ri_ref, rf_ref, cnt_ref, carry_ref):
    tb = ROUTER_TB

    @pl.when(pl.program_id(0) == 0)
    def _():
        carry_ref[...] = jnp.zeros_like(carry_ref)

    x = x_ref[...]
    ms = jnp.mean(x * x, axis=-1, keepdims=True)
    h = x * lax.rsqrt(ms + RMS_EPS) * nw_ref[...]

    lo = h[:, :HALF_D].astype(BF16).astype(F32)
    hi = h[:, HALF_D:].astype(BF16).astype(F32)
    lo_b = lax.shift_right_logical(lax.bitcast_convert_type(lo, U32), jnp.uint32(16))
    hi_b = lax.bitcast_convert_type(hi, U32) & jnp.uint32(0xFFFF0000)
    hp_ref[...] = hi_b | lo_b

    logits = jnp.dot(h, wr_ref[...], preferred_element_type=F32,
                     precision=lax.Precision.HIGHEST) + br_ref[...]
    lane = lax.broadcasted_iota(I32, (tb, LANES), 1)
    vals, idxs = [], []
    cur = logits
    for _ in range(TOP_K):
        m = jnp.max(cur, axis=-1, keepdims=True)
        idx = jnp.min(jnp.where(cur == m, lane, LANES), axis=-1, keepdims=True)
        vals.append(m)
        idxs.append(idx)
        cur = jnp.where(lane == idx, -jnp.inf, cur)

    exps = [jnp.exp(v - vals[0]) for v in vals]
    denom = exps[0] + exps[1] + exps[2] + exps[3]
    gates = [e / denom for e in exps]

    onehot = jnp.zeros((tb, LANES), F32)
    for idx in idxs:
        onehot = onehot + (lane == idx).astype(F32)
    r_i = lax.broadcasted_iota(I32, (tb, tb), 0)
    c_i = lax.broadcasted_iota(I32, (tb, tb), 1)
    tri = (r_i > c_i).astype(BF16)
    carry = carry_ref[0:1, :]
    cum = _dot(tri, onehot.astype(BF16)) + carry
    carry_new = carry + jnp.sum(onehot, axis=0, keepdims=True)
    carry_ref[...] = jnp.broadcast_to(carry_new, carry_ref.shape)
    cnt_ref[...] = jnp.broadcast_to(carry_new, cnt_ref.shape)

    ri = jnp.zeros((tb, LANES), I32)
    rf = jnp.zeros((tb, LANES), F32)
    for k in range(TOP_K):
        rank = jnp.sum(jnp.where(lane == idxs[k], cum, 0.0), axis=-1, keepdims=True).astype(I32)
        ri = jnp.where(lane == k, idxs[k], ri)
        ri = jnp.where(lane == TOP_K + k, rank, ri)
        rf = jnp.where(lane == k, gates[k], rf)
    ri_ref[...] = ri
    rf_ref[...] = rf


def router(x2, nw, w_router, b_router):
    t, d = x2.shape
    tb = ROUTER_TB
    wr = jnp.zeros((d, LANES), F32).at[:, :N_EXPERTS].set(w_router)
    br = jnp.full((1, LANES), -jnp.inf, F32).at[0, :N_EXPERTS].set(b_router)
    return pl.pallas_call(
        _router_kernel,
        grid=(t // tb,),
        in_specs=[pl.BlockSpec((tb, d), lambda i: (i, 0)),
                  pl.BlockSpec((1, d), lambda i: (0, 0)),
                  pl.BlockSpec((d, LANES), lambda i: (0, 0)),
                  pl.BlockSpec((1, LANES), lambda i: (0, 0))],
        out_specs=[pl.BlockSpec((tb, HALF_D), lambda i: (i, 0)),
                   pl.BlockSpec((tb, LANES), lambda i: (i, 0)),
                   pl.BlockSpec((tb, LANES), lambda i: (i, 0)),
                   pl.BlockSpec((SUBLANES, LANES), lambda i: (0, 0))],
        out_shape=[jax.ShapeDtypeStruct((t, HALF_D), U32),
                   jax.ShapeDtypeStruct((t, LANES), I32),
                   jax.ShapeDtypeStruct((t, LANES), F32),
                   jax.ShapeDtypeStruct((SUBLANES, LANES), F32)],
        scratch_shapes=[pltpu.VMEM((SUBLANES, LANES), F32)],
        compiler_params=_cparams(("arbitrary",)),
        name="moe_router",
    )(x2, nw.reshape(1, d), wr, br)


DISPATCH_TB = 256


def _dispatch_kernel(dest_ref, hp_ref, xs_init_ref, xs_ref, sem):
    del xs_init_ref
    tb = DISPATCH_TB

    def body(r, c):
        for k in range(TOP_K):
            d = dest_ref[0, 0, r * TOP_K + k]
            pltpu.make_async_copy(hp_ref.at[pl.ds(r, 1)], xs_ref.at[pl.ds(d, 1)], sem).start()
        return c

    lax.fori_loop(0, tb, body, 0)
    for k in range(TOP_K):
        pltpu.make_async_copy(hp_ref, xs_ref.at[pl.ds(0, tb)], sem).wait()


def dispatch(hp, dest_blocks, n_rows):
    t, w = hp.shape
    tb = DISPATCH_TB
    return pl.pallas_call(
        _dispatch_kernel,
        grid=(t // tb,),
        in_specs=[pl.BlockSpec((1, 1, tb * TOP_K), lambda i: (i, 0, 0), memory_space=pltpu.SMEM),
                  pl.BlockSpec((tb, w), lambda i: (i, 0)),
                  pl.BlockSpec(memory_space=pl.ANY)],
        out_specs=pl.BlockSpec(memory_space=pl.ANY),
        out_shape=jax.ShapeDtypeStruct((n_rows, w), hp.dtype),
        scratch_shapes=[pltpu.SemaphoreType.DMA],
        input_output_aliases={2: 0},
        compiler_params=_cparams(("arbitrary",)),
        name="moe_dispatch",
    )(dest_blocks, hp, jnp.zeros((n_rows, w), hp.dtype))


def _moe_ffn_kernel(te_ref, tb_ref, nr_ref, x_ref, wg_ref, wu_ref, wd_ref, bg_ref, bu_ref, bd_ref, o_ref,
                    xlo_ref, xhi_ref, wgu_ref, wdb_ref, act_ref):
    j = pl.program_id(0)
    c = pl.program_id(1)
    nrows = nr_ref[j]
    full = nrows > TILE_M - SUB_M

    def activations(rows):
        gu = _dot(xlo_ref[rows, :], wgu_ref[0:HALF_D, :])
        gu = gu + _dot(xhi_ref[rows, :], wgu_ref[HALF_D:D_MODEL, :])
        g = jnp.minimum(gu[:, 0:FF_CHUNK] + bg_ref[...], SWIGLU_LIMIT)
        u = jnp.clip(gu[:, FF_CHUNK:2 * FF_CHUNK] + bu_ref[...], -SWIGLU_LIMIT, SWIGLU_LIMIT)
        act_ref[c, rows, :] = ((u + 1.0) * g * jax.nn.sigmoid(SWIGLU_ALPHA * g)).astype(BF16)

    def down(rows):
        acc = _dot(act_ref[0, rows, :], wdb_ref[0:FF_CHUNK, :]) + bd_ref[...]
        for cc in range(1, N_FF_CHUNKS):
            acc = acc + _dot(act_ref[cc, rows, :], wdb_ref[cc * FF_CHUNK:(cc + 1) * FF_CHUNK, :])
        o_ref[rows, :] = acc

    def per_sub_block(fn):
        @pl.when(full)
        def _():
            fn(pl.ds(0, TILE_M))

        for s in range(N_SUB):
            @pl.when(jnp.logical_and(jnp.logical_not(full), s * SUB_M < nrows))
            def _():
                fn(pl.ds(s * SUB_M, SUB_M))

    @pl.when(jnp.logical_and(nrows > 0, c < N_FF_CHUNKS))
    def _():
        @pl.when(c == 0)
        def _():
            for s in range(N_SUB):
                rows = pl.ds(s * SUB_M, SUB_M)
                w = x_ref[rows, :]
                lo = lax.bitcast_convert_type(lax.shift_left(w, jnp.uint32(16)), F32)
                hi = lax.bitcast_convert_type(w & jnp.uint32(0xFFFF0000), F32)
                xlo_ref[rows, :] = lo.astype(BF16)
                xhi_ref[rows, :] = hi.astype(BF16)

        def cast_body(r, carry):
            sl = pl.ds(pl.multiple_of(r * 256, 256), 256)
            wgu_ref[sl, 0:FF_CHUNK] = wg_ref[sl, :].astype(BF16)
            wgu_ref[sl, FF_CHUNK:2 * FF_CHUNK] = wu_ref[sl, :].astype(BF16)
            return carry

        lax.fori_loop(0, D_MODEL // 256, cast_body, 0)
        per_sub_block(activations)

    @pl.when(jnp.logical_and(nrows > 0, c >= N_FF_CHUNKS))
    def _():
        def cast_body(r, carry):
            sl = pl.ds(pl.multiple_of(r * 256, 256), 256)
            wdb_ref[sl, :] = wd_ref[sl, :].astype(BF16)
            return carry

        lax.fori_loop(0, EXPERT_FF // 256, cast_body, 0)
        per_sub_block(down)
        for s in range(N_SUB):
            @pl.when(s * SUB_M >= nrows)
            def _():
                o_ref[pl.ds(s * SUB_M, SUB_M), :] = jnp.zeros((SUB_M, D_CHUNK), F32)

    @pl.when(jnp.logical_and(nrows == 0, c >= N_FF_CHUNKS))
    def _():
        o_ref[...] = jnp.zeros(o_ref.shape, F32)


def moe_ffn(xs, tile_expert, tile_block, tile_rows, layer, w_gate_up, b_gate_up, w_down, b_down):
    n_rows = xs.shape[0]
    depth = w_gate_up.shape[0]
    bgu = b_gate_up.reshape(depth, N_EXPERTS, 1, 2 * EXPERT_FF)
    bd = b_down.reshape(depth, N_EXPERTS, 1, D_MODEL)
    last_ff = N_FF_CHUNKS - 1

    def ff_chunk(c):
        return jnp.minimum(c, last_ff)

    def d_chunk(c):
        return jnp.maximum(c - N_FF_CHUNKS, 0)

    grid_spec = pltpu.PrefetchScalarGridSpec(
        num_scalar_prefetch=3,
        grid=(n_rows // TILE_M, N_FF_CHUNKS + N_D_CHUNKS),
        in_specs=[
            pl.BlockSpec((TILE_M, HALF_D), lambda j, c, te, tb, nr: (tb[j], 0)),
            pl.BlockSpec((None, None, D_MODEL, FF_CHUNK), lambda j, c, te, tb, nr: (layer, te[j], 0, ff_chunk(c))),
            pl.BlockSpec((None, None, D_MODEL, FF_CHUNK),
                         lambda j, c, te, tb, nr: (layer, te[j], 0, N_FF_CHUNKS + ff_chunk(c))),
            pl.BlockSpec((None, None, EXPERT_FF, D_CHUNK), lambda j, c, te, tb, nr: (layer, te[j], 0, d_chunk(c))),
            pl.BlockSpec((None, None, 1, FF_CHUNK), lambda j, c, te, tb, nr: (layer, te[j], 0, ff_chunk(c))),
            pl.BlockSpec((None, None, 1, FF_CHUNK),
                         lambda j, c, te, tb, nr: (layer, te[j], 0, N_FF_CHUNKS + ff_chunk(c))),
            pl.BlockSpec((None, None, 1, D_CHUNK), lambda j, c, te, tb, nr: (layer, te[j], 0, d_chunk(c))),
        ],
        out_specs=pl.BlockSpec((TILE_M, D_CHUNK), lambda j, c, te, tb, nr: (j, d_chunk(c))),
        scratch_shapes=[pltpu.VMEM((TILE_M, HALF_D), BF16),
                        pltpu.VMEM((TILE_M, HALF_D), BF16),
                        pltpu.VMEM((D_MODEL, 2 * FF_CHUNK), BF16),
                        pltpu.VMEM((EXPERT_FF, D_CHUNK), BF16),
                        pltpu.VMEM((N_FF_CHUNKS, TILE_M, FF_CHUNK), BF16)],
    )
    return pl.pallas_call(
        _moe_ffn_kernel,
        grid_spec=grid_spec,
        out_shape=jax.ShapeDtypeStruct((n_rows, D_MODEL), F32),
        compiler_params=_cparams(("arbitrary", "arbitrary")),
        name="moe_ffn",
    )(tile_expert, tile_block, tile_rows, xs, w_gate_up, w_gate_up, w_down, bgu, bgu, bd)


COMBINE_TB = 256


def _combine_kernel(dest_ref, x_ref, g_ref, ys_ref, o_ref, buf_ref, sem):
    tb = COMBINE_TB

    def body(r, c):
        for k in range(TOP_K):
            d = dest_ref[0, 0, r * TOP_K + k]
            pltpu.make_async_copy(ys_ref.at[pl.ds(d, 1)], buf_ref.at[k, pl.ds(r, 1)], sem).start()
        return c

    lax.fori_loop(0, tb, body, 0)
    for k in range(TOP_K):
        pltpu.make_async_copy(ys_ref.at[pl.ds(0, tb)], buf_ref.at[k], sem).wait()
    acc = x_ref[...]
    g = g_ref[...]
    for k in range(TOP_K):
        acc = acc + g[:, k:k + 1] * buf_ref[k]
    o_ref[...] = acc


def combine(x2, gates, ys, dest_blocks):
    t, d = x2.shape
    tb = COMBINE_TB
    return pl.pallas_call(
        _combine_kernel,
        grid=(t // tb,),
        in_specs=[pl.BlockSpec((1, 1, tb * TOP_K), lambda i: (i, 0, 0), memory_space=pltpu.SMEM),
                  pl.BlockSpec((tb, d), lambda i: (i, 0)),
                  pl.BlockSpec((tb, LANES), lambda i: (i, 0)),
                  pl.BlockSpec(memory_space=pl.ANY)],
        out_specs=pl.BlockSpec((tb, d), lambda i: (i, 0)),
        out_shape=jax.ShapeDtypeStruct((t, d), F32),
        scratch_shapes=[pltpu.VMEM((TOP_K, tb, d), F32), pltpu.SemaphoreType.DMA],
        compiler_params=_cparams(("arbitrary",)),
        name="moe_combine",
    )(dest_blocks, x2, gates, ys)


def moe_layer(x2, nw, w_router, b_router, layer, w_gate_up, b_gate_up, w_down, b_down):
    t = x2.shape[0]
    max_tiles = (t * TOP_K) // TILE_M + N_EXPERTS
    hp, ri, rf, cnt = router(x2, nw, w_router, b_router)
    idx = ri[:, 0:TOP_K]
    rank = ri[:, TOP_K:2 * TOP_K]
    counts = cnt[0, :N_EXPERTS].astype(I32)

    tiles_e = (counts + TILE_M - 1) // TILE_M
    tile_end = jnp.cumsum(tiles_e)
    tile_start = tile_end - tiles_e
    n_tiles = tile_end[-1]
    dest = (tile_start * TILE_M)[idx] + rank
    j = jnp.arange(max_tiles, dtype=I32)
    jc = jnp.minimum(j, n_tiles - 1)
    te = jnp.sum((jc[:, None] >= tile_end[None, :]).astype(I32), axis=1)
    rows_left = counts[te] - (jc - tile_start[te]) * TILE_M
    tile_rows = jnp.where(j < n_tiles, jnp.clip(rows_left, 0, TILE_M), 0).astype(I32)
    dest_blocks = dest.reshape(t // DISPATCH_TB, 1, DISPATCH_TB * TOP_K).astype(I32)

    xs = dispatch(hp, dest_blocks, max_tiles * TILE_M)
    ys = moe_ffn(xs, te.astype(I32), jc.astype(I32), tile_rows, layer, w_gate_up, b_gate_up, w_down, b_down)
    return combine(x2, rf, ys, dest_blocks)


def even_mixer(x2, bsz, seq, nw, w_in, conv_w, conv_b, dt_bias, a_log, d_skip, ssd_norm, q_norm, k_norm, w_out):
    d = x2.shape[1]
    o2 = SSD_INNER + SSD_INNER + SSD_BC
    o3 = o2 + SSD_HEADS
    w_cat = jnp.concatenate(
        [w_in[:, :o2], w_in[:, o3:], w_in[:, o2:o3], jnp.zeros((d, PROJ_TN - SSD_HEADS), F32)], axis=1).astype(BF16)
    proj = norm_matmul(x2, nw, w_cat)
    y_ssd = ssd_mixer(proj, conv_w, conv_b, dt_bias, a_log, d_skip, ssd_norm, bsz, seq,
                      z_col=0, x_col=1, bc_col=(2 * SSD_INNER) // SSD_BC, dt_col=(o2 + 3 * SB_INNER) // LANES)
    y_sb = sb_attention(proj, q_norm, k_norm, bsz, seq, q_col=o2 // LANES, k_col=(o2 + SB_INNER) // LANES,
                        v_col=(o2 + 2 * SB_INNER) // LANES)
    w_out_bf = w_out.astype(BF16)
    return matmul_residual([y_ssd, y_sb], [w_out_bf[:SSD_INNER], w_out_bf[SSD_INNER:]], x2)


def odd_mixer(x2, bsz, seq, nw, w_in, conv_w, dt_bias, a_log, out_norm, w_out):
    d = x2.shape[1]
    n_in = w_in.shape[1]
    n_pad = -n_in % PROJ_TN
    w_cat = jnp.concatenate([w_in, jnp.zeros((d, n_pad), F32)], axis=1).astype(BF16)
    proj = norm_matmul(x2, nw, w_cat)
    qw = GDN_KH_STEP * GDN_DIM
    o = gdn_mixer(proj, conv_w, dt_bias, a_log, out_norm, bsz, seq,
                  q_col=0, k_col=GDN_KEY // qw, v_col=(2 * GDN_KEY) // (2 * qw), z_col=(2 * GDN_KEY + GDN_VAL) // (2 * qw))
    return matmul_residual([o], [w_out.astype(BF16)], x2)


def kernel(x, mix_norm, ffn_norm, ev_w_in, ev_conv_w, ev_conv_b, ev_dt_bias, ev_a_log, ev_d_skip, ev_ssd_norm, ev_q_norm, ev_k_norm, ev_w_out, od_w_in, od_conv_w, od_dt_bias, od_a_log, od_out_norm, od_w_out, moe_w_router, moe_b_router, moe_w_gate_up, moe_b_gate_up, moe_w_down, moe_b_down):
    bsz, seq, d = x.shape
    depth = mix_norm.shape[0]
    x2 = x.reshape(bsz * seq, d)
    for layer in range(depth):
        i = layer // 2
        if layer % 2 == 0:
            x2 = even_mixer(x2, bsz, seq, mix_norm[layer], ev_w_in[i], ev_conv_w[i], ev_conv_b[i], ev_dt_bias[i],
                            ev_a_log[i], ev_d_skip[i], ev_ssd_norm[i], ev_q_norm[i], ev_k_norm[i], ev_w_out[i])
        else:
            x2 = odd_mixer(x2, bsz, seq, mix_norm[layer], od_w_in[i], od_conv_w[i], od_dt_bias[i], od_a_log[i],
                           od_out_norm[i], od_w_out[i])
        x2 = moe_layer(x2, ffn_norm[layer], moe_w_router[layer], moe_b_router[layer], layer,
                       moe_w_gate_up, moe_b_gate_up, moe_w_down, moe_b_down)
    return x2.reshape(bsz, seq, d)
```

```python
import jax
import jax.numpy as jnp
from jax import lax
from jax.experimental import pallas as pl
from jax.experimental.pallas import tpu as pltpu

F32 = jnp.float32
BF16 = jnp.bfloat16
I32 = jnp.int32
U32 = jnp.uint32

D_MODEL = 2048
RMS_EPS = 1e-6
CONV_K = 4

SSD_HEADS = 32
SSD_HEAD_DIM = 64
SSD_INNER = 2048
SSD_GROUPS = 4
SSD_HPG = 8
SSD_STATE = 128
SSD_CHUNK = 128
SSD_GW = SSD_HPG * SSD_HEAD_DIM
SSD_BC = 2 * SSD_GROUPS * SSD_STATE

SB_HEADS = 16
SB_HEAD_DIM = 128
SB_INNER = 2048
SB_TQ = 256
SB_TK = 256
SB_EXIT = -104.0

GDN_K_HEADS = 16
GDN_V_HEADS = 32
GDN_DIM = 128
GDN_KEY = GDN_K_HEADS * GDN_DIM
GDN_VAL = GDN_V_HEADS * GDN_DIM
GDN_CHUNK = 64
GDN_RB = 256
GDN_A_LANE = GDN_V_HEADS
GDN_KH_STEP = 2

N_EXPERTS = 32
TOP_K = 4
EXPERT_FF = 1024
SWIGLU_LIMIT = 7.0
SWIGLU_ALPHA = 1.702

LANES = 128
SUBLANES = 8
VMEM_LIMIT = 56 * 1024 * 1024

TILE_M = 1024
SUB_M = 256
N_SUB = TILE_M // SUB_M
FF_CHUNK = 256
N_FF_CHUNKS = EXPERT_FF // FF_CHUNK
D_CHUNK = 512
N_D_CHUNKS = D_MODEL // D_CHUNK
HALF_D = D_MODEL // 2
PROJ_TN = 512

HALO = SUBLANES


def _cparams(sem):
    return pltpu.CompilerParams(dimension_semantics=sem, vmem_limit_bytes=VMEM_LIMIT)


def _dot(a, b):
    return jnp.dot(a, b, preferred_element_type=F32)


def _dot_nt(a, b):
    return lax.dot_general(a, b, (((1,), (1,)), ((), ())), preferred_element_type=F32)


def _split_bf16(x):
    hi = x.astype(BF16)
    lo = (x - hi.astype(F32)).astype(BF16)
    return hi, lo


def _dot_hilo(x, m_bf):
    hi, lo = _split_bf16(x)
    return _dot(hi, m_bf) + _dot(lo, m_bf)


def _dot_hilo_lhs(m_bf, x):
    hi, lo = _split_bf16(x)
    return _dot(m_bf, hi) + _dot(m_bf, lo)


def _softplus(x):
    return jnp.maximum(x, 0.0) + jnp.log1p(jnp.exp(-jnp.abs(x)))


def _silu(x):
    return x * jax.nn.sigmoid(x)


def _norm_matmul_kernel(x_ref, nw_ref, w_ref, o_ref, h_ref):
    @pl.when(pl.program_id(1) == 0)
    def _():
        rows = 128

        def body(r, c):
            sl = pl.ds(pl.multiple_of(r * rows, rows), rows)
            x = x_ref[sl, :]
            ms = jnp.mean(x * x, axis=-1, keepdims=True)
            h_ref[sl, :] = (x * lax.rsqrt(ms + RMS_EPS) * nw_ref[...]).astype(BF16)
            return c

        lax.fori_loop(0, x_ref.shape[0] // rows, body, 0)

    o_ref[...] = _dot(h_ref[...], w_ref[...])


def norm_matmul(x2, nw, w_bf, tm=1024, tn=PROJ_TN):
    t, d = x2.shape
    n = w_bf.shape[1]
    return pl.pallas_call(
        _norm_matmul_kernel,
        grid=(t // tm, n // tn),
        in_specs=[pl.BlockSpec((tm, d), lambda i, j: (i, 0)),
                  pl.BlockSpec((1, d), lambda i, j: (0, 0)),
                  pl.BlockSpec((d, tn), lambda i, j: (0, j))],
        out_specs=pl.BlockSpec((tm, tn), lambda i, j: (i, j)),
        out_shape=jax.ShapeDtypeStruct((t, n), F32),
        scratch_shapes=[pltpu.VMEM((tm, d), BF16)],
        compiler_params=_cparams(("parallel", "arbitrary")),
        name="norm_matmul",
    )(x2, nw.reshape(1, d), w_bf)


def _matmul_residual_kernel(*refs):
    n_in = (len(refs) - 2) // 2
    x_ref, o_ref = refs[2 * n_in], refs[2 * n_in + 1]
    acc = x_ref[...]
    for i in range(n_in):
        acc = acc + _dot(refs[i][...], refs[n_in + i][...])
    o_ref[...] = acc


def matmul_residual(ys, ws, x2, tm=1024, tn=512):
    t, n = x2.shape
    in_specs = [pl.BlockSpec((tm, y.shape[1]), lambda i, j: (i, 0)) for y in ys]
    in_specs += [pl.BlockSpec((w.shape[0], tn), lambda i, j: (0, j)) for w in ws]
    in_specs += [pl.BlockSpec((tm, tn), lambda i, j: (i, j))]
    return pl.pallas_call(
        _matmul_residual_kernel,
        grid=(t // tm, n // tn),
        in_specs=in_specs,
        out_specs=pl.BlockSpec((tm, tn), lambda i, j: (i, j)),
        out_shape=jax.ShapeDtypeStruct((t, n), F32),
        compiler_params=_cparams(("parallel", "parallel")),
        name="matmul_residual",
    )(*ys, *ws, x2)


def _sb_kernel(q_ref, k_ref, v_ref, qn_ref, kn_ref, u_ref, o_ref, kb_ref, vb_ref, r_ref, acc_ref, flag_ref):
    qi = pl.program_id(2)
    seq = k_ref.shape[0]

    @pl.when(qi == 0)
    def _():
        def body(i, c):
            sl = pl.ds(pl.multiple_of(i * SB_TK, SB_TK), SB_TK)
            k = k_ref[sl, :]
            kb_ref[sl, :] = (k * lax.rsqrt(jnp.mean(k * k, axis=-1, keepdims=True) + RMS_EPS)
                             * kn_ref[...]).astype(BF16)
            vb_ref[sl, :] = v_ref[sl, :].astype(BF16)
            return c

        lax.fori_loop(0, seq // SB_TK, body, 0)

    q = q_ref[...]
    qn = (q * lax.rsqrt(jnp.mean(q * q, axis=-1, keepdims=True) + RMS_EPS) * qn_ref[...]
          * (SB_HEAD_DIM ** -0.5)).astype(BF16)
    r_ref[...] = jnp.zeros_like(r_ref)
    acc_ref[...] = jnp.zeros_like(acc_ref)
    flag_ref[0] = 0
    row = lax.broadcasted_iota(I32, (SB_TQ, SB_TK), 0) + qi * SB_TQ
    col0 = lax.broadcasted_iota(I32, (SB_TQ, SB_TK), 1)

    def body(it, c):
        kb = qi - it

        @pl.when(flag_ref[0] == 0)
        def _():
            ks = pl.ds(pl.multiple_of(kb * SB_TK, SB_TK), SB_TK)
            s = _dot_nt(qn, kb_ref[ks, :])
            valid = (col0 + kb * SB_TK) < row
            ls = jnp.minimum(s, 0.0) - jnp.log1p(jnp.exp(-jnp.abs(s)))
            lk = jnp.where(valid, ls - s, 0.0)
            bw = _dot_hilo(lk, u_ref[...])
            r = r_ref[...]
            p = jnp.where(valid, jnp.exp(ls + bw + r), 0.0)
            acc_ref[...] += _dot(p.astype(BF16), vb_ref[ks, :])
            r_new = r + jnp.sum(lk, axis=-1, keepdims=True)
            r_ref[...] = r_new
            flag_ref[0] = (jnp.max(r_new) < SB_EXIT).astype(I32)

        return c

    lax.fori_loop(0, qi + 1, body, 0)
    o_ref[...] = acc_ref[...].astype(o_ref.dtype)


def sb_attention(proj, q_norm, k_norm, bsz, seq, q_col, k_col, v_col):
    nq = seq // SB_TQ
    r_i = lax.broadcasted_iota(I32, (SB_TK, SB_TK), 0)
    c_i = lax.broadcasted_iota(I32, (SB_TK, SB_TK), 1)
    later = (r_i > c_i).astype(BF16)
    return pl.pallas_call(
        _sb_kernel,
        grid=(bsz, SB_HEADS, nq),
        in_specs=[pl.BlockSpec((SB_TQ, SB_HEAD_DIM), lambda b, h, i: (b * nq + i, q_col + h)),
                  pl.BlockSpec((seq, SB_HEAD_DIM), lambda b, h, i: (b, k_col + h)),
                  pl.BlockSpec((seq, SB_HEAD_DIM), lambda b, h, i: (b, v_col + h)),
                  pl.BlockSpec((1, SB_HEAD_DIM), lambda b, h, i: (0, 0)),
                  pl.BlockSpec((1, SB_HEAD_DIM), lambda b, h, i: (0, 0)),
                  pl.BlockSpec((SB_TK, SB_TK), lambda b, h, i: (0, 0))],
        out_specs=pl.BlockSpec((SB_TQ, SB_HEAD_DIM), lambda b, h, i: (b * nq + i, h)),
        out_shape=jax.ShapeDtypeStruct((bsz * seq, SB_INNER), BF16),
        scratch_shapes=[pltpu.VMEM((seq, SB_HEAD_DIM), BF16),
                        pltpu.VMEM((seq, SB_HEAD_DIM), BF16),
                        pltpu.VMEM((SB_TQ, 1), F32),
                        pltpu.VMEM((SB_TQ, SB_HEAD_DIM), F32),
                        pltpu.SMEM((1,), I32)],
        compiler_params=_cparams(("parallel", "parallel", "arbitrary")),
        name="sb_attention",
    )(proj, proj, proj, q_norm.reshape(1, -1), k_norm.reshape(1, -1), later)


def _conv_silu(ext_ref, w_ref, b, rows):
    acc = b
    for k in range(CONV_K):
        acc = acc + ext_ref[pl.ds(HALO - (CONV_K - 1) + k, rows), :] * w_ref[k:k + 1, :]
    return _silu(acc)


def _ssd_kernel(z_ref, x_ref, bc_ref, dt_ref, cwx_ref, cwbc_ref, cbx_ref, cbbc_ref, dtb_ref, alog_ref, dskip_ref,
                nw_ref, e_ref, o_ref, extx_ref, extbc_ref, h_ref, y_ref):
    c = pl.program_id(1)
    L = SSD_CHUNK

    @pl.when(c == 0)
    def _():
        extx_ref[0:HALO, :] = jnp.zeros((HALO, extx_ref.shape[1]), F32)
        extbc_ref[0:HALO, :] = jnp.zeros((HALO, extbc_ref.shape[1]), F32)
        h_ref[...] = jnp.zeros_like(h_ref)

    @pl.when(c > 0)
    def _():
        extx_ref[0:HALO, :] = extx_ref[L:L + HALO, :]
        extbc_ref[0:HALO, :] = extbc_ref[L:L + HALO, :]

    extx_ref[HALO:HALO + L, :] = x_ref[...]
    extbc_ref[HALO:HALO + L, :] = bc_ref[...]
    xs = _conv_silu(extx_ref, cwx_ref, cbx_ref[...], L)
    bc = _conv_silu(extbc_ref, cwbc_ref, cbbc_ref[...], L)

    dt = _softplus(dt_ref[...] + dtb_ref[...])
    log_a = -jnp.exp(alog_ref[...]) * dt
    r_i = lax.broadcasted_iota(I32, (L, L), 0)
    c_i = lax.broadcasted_iota(I32, (L, L), 1)
    causal = r_i >= c_i
    a_cs = _dot_hilo_lhs(causal.astype(BF16), log_a)
    a_cs_t = a_cs.T
    dt_full = _dot_hilo(dt, e_ref[...])
    acs_full = _dot_hilo(a_cs, e_ref[...])
    acs_last = acs_full[L - 1:L, :]
    xdt = xs * dt_full
    x_end = (xdt * jnp.exp(acs_last - acs_full)).astype(BF16)
    xdt_b = xdt.astype(BF16)

    for g in range(SSD_GROUPS):
        b_g = bc[:, g * SSD_STATE:(g + 1) * SSD_STATE]
        c_g = bc[:, (SSD_GROUPS + g) * SSD_STATE:(SSD_GROUPS + g + 1) * SSD_STATE].astype(BF16)
        cb = _dot_nt(c_g, b_g.astype(BF16))
        gsl = slice(g * SSD_GW, (g + 1) * SSD_GW)
        h_g = h_ref[g]
        y_off = _dot(c_g, h_g.astype(BF16)) * jnp.exp(acs_full[:, gsl])
        for r in range(SSD_HPG):
            h = g * SSD_HPG + r
            diff = a_cs[:, h:h + 1] - a_cs_t[h:h + 1, :]
            scores = (cb * jnp.where(causal, jnp.exp(diff), 0.0)).astype(BF16)
            hsl = slice(h * SSD_HEAD_DIM, (h + 1) * SSD_HEAD_DIM)
            y_ref[:, hsl] = _dot(scores, xdt_b[:, hsl]) + y_off[:, r * SSD_HEAD_DIM:(r + 1) * SSD_HEAD_DIM]
        states = _dot(b_g.T.astype(BF16), x_end[:, gsl])
        h_ref[g] = h_g * jnp.exp(acs_last[:, gsl]) + states

    y = y_ref[...] + xs * dskip_ref[...]
    y = y * _silu(z_ref[...])
    o_ref[...] = (y * lax.rsqrt(jnp.mean(y * y, axis=-1, keepdims=True) + RMS_EPS) * nw_ref[...]).astype(o_ref.dtype)


def ssd_mixer(proj, conv_w, conv_b, dt_bias, a_log, d_skip, ssd_norm, bsz, seq, z_col, x_col, bc_col, dt_col):
    nc = seq // SSD_CHUNK
    L = SSD_CHUNK
    pad = LANES - SSD_HEADS
    dtb = jnp.pad(dt_bias, (0, pad)).reshape(1, LANES)
    alog = jnp.pad(a_log, (0, pad)).reshape(1, LANES)
    dskip = jnp.repeat(d_skip, SSD_HEAD_DIM).reshape(1, SSD_INNER)
    head_of_col = jnp.arange(SSD_INNER, dtype=I32) // SSD_HEAD_DIM
    expand = (jnp.arange(LANES, dtype=I32)[:, None] == head_of_col[None, :]).astype(BF16)
    cwx, cwbc = conv_w[:, :SSD_INNER], conv_w[:, SSD_INNER:]
    cbx, cbbc = conv_b[:SSD_INNER].reshape(1, -1), conv_b[SSD_INNER:].reshape(1, -1)

    def const(b, c):
        return (0, 0)

    return pl.pallas_call(
        _ssd_kernel,
        grid=(bsz, nc),
        in_specs=[pl.BlockSpec((L, SSD_INNER), lambda b, c: (b * nc + c, z_col)),
                  pl.BlockSpec((L, SSD_INNER), lambda b, c: (b * nc + c, x_col)),
                  pl.BlockSpec((L, SSD_BC), lambda b, c: (b * nc + c, bc_col)),
                  pl.BlockSpec((L, LANES), lambda b, c: (b * nc + c, dt_col)),
                  pl.BlockSpec((CONV_K, SSD_INNER), const),
                  pl.BlockSpec((CONV_K, SSD_BC), const),
                  pl.BlockSpec((1, SSD_INNER), const),
                  pl.BlockSpec((1, SSD_BC), const),
                  pl.BlockSpec((1, LANES), const),
                  pl.BlockSpec((1, LANES), const),
                  pl.BlockSpec((1, SSD_INNER), const),
                  pl.BlockSpec((1, SSD_INNER), const),
                  pl.BlockSpec((LANES, SSD_INNER), const)],
        out_specs=pl.BlockSpec((L, SSD_INNER), lambda b, c: (b * nc + c, 0)),
        out_shape=jax.ShapeDtypeStruct((bsz * seq, SSD_INNER), BF16),
        scratch_shapes=[pltpu.VMEM((HALO + L, SSD_INNER), F32),
                        pltpu.VMEM((HALO + L, SSD_BC), F32),
                        pltpu.VMEM((SSD_GROUPS, SSD_STATE, SSD_GW), F32),
                        pltpu.VMEM((L, SSD_INNER), F32)],
        compiler_params=_cparams(("parallel", "arbitrary")),
        name="ssd_mixer",
    )(proj, proj, proj, proj, cwx, cwbc, cbx, cbbc, dtb, alog, dskip, ssd_norm.reshape(1, -1), expand)


def _gdn_kernel(q_ref, k_ref, v_ref, z_ref, ba_ref, cwq_ref, cwk_ref, cwv_ref, alog_ref, dtb_ref, nw_ref, tril_ref,
                o_ref, extq_ref, extk_ref, extv_ref, s_ref, gt_ref):
    hp = pl.program_id(1)
    i = pl.program_id(2)
    R = GDN_RB
    C = GDN_CHUNK
    D = GDN_DIM

    @pl.when(i == 0)
    def _():
        extq_ref[0:HALO, :] = jnp.zeros((HALO, extq_ref.shape[1]), F32)
        extk_ref[0:HALO, :] = jnp.zeros((HALO, extk_ref.shape[1]), F32)
        extv_ref[0:HALO, :] = jnp.zeros((HALO, extv_ref.shape[1]), F32)
        s_ref[...] = jnp.zeros_like(s_ref)

    @pl.when(i > 0)
    def _():
        extq_ref[0:HALO, :] = extq_ref[R:R + HALO, :]
        extk_ref[0:HALO, :] = extk_ref[R:R + HALO, :]
        extv_ref[0:HALO, :] = extv_ref[R:R + HALO, :]

    extq_ref[HALO:HALO + R, :] = q_ref[...]
    extk_ref[HALO:HALO + R, :] = k_ref[...]
    extv_ref[HALO:HALO + R, :] = v_ref[...]
    no_bias = jnp.zeros((1, 1), F32)
    q_all = _conv_silu(extq_ref, cwq_ref, no_bias, R)
    k_all = _conv_silu(extk_ref, cwk_ref, no_bias, R)
    v_all = _conv_silu(extv_ref, cwv_ref, no_bias, R)

    ba = ba_ref[...]
    beta_all = jax.nn.sigmoid(ba)
    g_all = -jnp.exp(alog_ref[...]) * _softplus(ba + dtb_ref[...])
    gcs_all = _dot_hilo_lhs(tril_ref[...], g_all)
    gt_ref[...] = gcs_all.T

    r_i = lax.broadcasted_iota(I32, (R, R), 0)
    c_i = lax.broadcasted_iota(I32, (R, R), 1)
    same = (r_i // C) == (c_i // C)
    incl = jnp.logical_and(same, r_i >= c_i)
    strict = jnp.logical_and(same, r_i > c_i)
    eye = (r_i == c_i).astype(F32)
    lane = lax.broadcasted_iota(I32, (R, LANES), 1)

    for kh in range(GDN_KH_STEP):
        q = q_all[:, kh * D:(kh + 1) * D]
        k = k_all[:, kh * D:(kh + 1) * D]
        qn = q * lax.rsqrt(jnp.sum(q * q, axis=-1, keepdims=True) + RMS_EPS) * (D ** -0.5)
        kn = k * lax.rsqrt(jnp.sum(k * k, axis=-1, keepdims=True) + RMS_EPS)
        qn_b = qn.astype(BF16)
        kn_b = kn.astype(BF16)
        kn_t = kn.T
        qk = _dot_nt(qn_b, kn_b)
        kk = _dot_nt(kn_b, kn_b)
        for j in range(2):
            vh = 2 * kh + j
            hidx = 2 * GDN_KH_STEP * hp + vh
            beta = jnp.sum(jnp.where(lane == hidx, beta_all, 0.0), axis=-1, keepdims=True)
            gc = jnp.sum(jnp.where(lane == GDN_A_LANE + hidx, gcs_all, 0.0), axis=-1, keepdims=True)
            gr = gt_ref[pl.ds(GDN_A_LANE + hidx, 1), :]
            dec = jnp.where(incl, jnp.exp(gc - gr), 0.0)
            a = jnp.where(strict, kk * beta * dec, 0.0)
            p = eye - a
            m = a
            for _ in range(5):
                m_b = m.astype(BF16)
                m = _dot(m_b, m_b)
                p = p + _dot(p.astype(BF16), m.astype(BF16))
            eg = jnp.exp(gc)
            v_j = v_all[:, vh * D:(vh + 1) * D]
            rhs = jnp.concatenate([kn * (beta * eg), v_j * beta], axis=1).astype(BF16)
            wu = _dot(p.astype(BF16), rhs)
            w = wu[:, :D]
            u = wu[:, D:]
            wu_b = wu.astype(BF16)
            qkd = jnp.where(incl, qk * dec, 0.0).astype(BF16)
            qd = qn * eg
            kw_ku, g_last = [], []
            for cc in range(R // C):
                rows = slice(cc * C, (cc + 1) * C)
                last = (cc + 1) * C - 1
                gl = gc[last:last + 1, :]
                k_end_t = (kn_t[:, rows] * jnp.exp(gl - gr[:, rows])).astype(BF16)
                g_last.append(gl)
                kw_ku.append(_dot(k_end_t, wu_b[rows]))
            outs = []
            for cc in range(R // C):
                rows = slice(cc * C, (cc + 1) * C)
                s_mat = s_ref[vh]
                s_b = s_mat.astype(BF16)
                ws_qs = _dot(jnp.concatenate([w[rows], qd[rows]], axis=0).astype(BF16), s_b)
                v_new = u[rows] - ws_qs[:C]
                outs.append(ws_qs[C:] + _dot(qkd[rows, rows], v_new.astype(BF16)))
                s_ref[vh] = (s_mat * jnp.exp(g_last[cc]) - _dot(kw_ku[cc][:, :D].astype(BF16), s_b)
                             + kw_ku[cc][:, D:])
            o = jnp.concatenate(outs, axis=0)
            o = o * lax.rsqrt(jnp.mean(o * o, axis=-1, keepdims=True) + RMS_EPS) * nw_ref[...]
            hs = slice(vh * D, (vh + 1) * D)
            o_ref[:, hs] = (o * _silu(z_ref[:, hs])).astype(o_ref.dtype)


def gdn_mixer(proj, conv_w, dt_bias, a_log, out_norm, bsz, seq, q_col, k_col, v_col, z_col):
    R = GDN_RB
    nr = seq // R
    qw = GDN_KH_STEP * GDN_DIM
    vw = 2 * qw
    ba_col = (2 * GDN_KEY + 2 * GDN_VAL) // LANES
    cwq, cwk, cwv = conv_w[:, :GDN_KEY], conv_w[:, GDN_KEY:2 * GDN_KEY], conv_w[:, 2 * GDN_KEY:]
    lane_pad = LANES - GDN_A_LANE - GDN_V_HEADS
    alog = jnp.pad(a_log, (GDN_A_LANE, lane_pad)).reshape(1, LANES)
    dtb = jnp.pad(dt_bias, (GDN_A_LANE, lane_pad)).reshape(1, LANES)
    r_i = lax.broadcasted_iota(I32, (R, R), 0)
    c_i = lax.broadcasted_iota(I32, (R, R), 1)
    tril = jnp.logical_and(r_i // GDN_CHUNK == c_i // GDN_CHUNK, r_i >= c_i).astype(BF16)

    def const(b, h, i):
        return (0, 0)

    return pl.pallas_call(
        _gdn_kernel,
        grid=(bsz, GDN_K_HEADS // GDN_KH_STEP, nr),
        in_specs=[pl.BlockSpec((R, qw), lambda b, h, i: (b * nr + i, q_col + h)),
                  pl.BlockSpec((R, qw), lambda b, h, i: (b * nr + i, k_col + h)),
                  pl.BlockSpec((R, vw), lambda b, h, i: (b * nr + i, v_col + h)),
                  pl.BlockSpec((R, vw), lambda b, h, i: (b * nr + i, z_col + h)),
                  pl.BlockSpec((R, LANES), lambda b, h, i: (b * nr + i, ba_col)),
                  pl.BlockSpec((CONV_K, qw), lambda b, h, i: (0, h)),
                  pl.BlockSpec((CONV_K, qw), lambda b, h, i: (0, h)),
                  pl.BlockSpec((CONV_K, vw), lambda b, h, i: (0, h)),
                  pl.BlockSpec((1, LANES), const),
                  pl.BlockSpec((1, LANES), const),
                  pl.BlockSpec((1, GDN_DIM), const),
                  pl.BlockSpec((R, R), const)],
        out_specs=pl.BlockSpec((R, vw), lambda b, h, i: (b * nr + i, h)),
        out_shape=jax.ShapeDtypeStruct((bsz * seq, GDN_VAL), BF16),
        scratch_shapes=[pltpu.VMEM((HALO + R, qw), F32),
                        pltpu.VMEM((HALO + R, qw), F32),
                        pltpu.VMEM((HALO + R, vw), F32),
                        pltpu.VMEM((2 * GDN_KH_STEP, GDN_DIM, GDN_DIM), F32),
                        pltpu.VMEM((LANES, R), F32)],
        compiler_params=_cparams(("parallel", "parallel", "arbitrary")),
        name="gdn_mixer",
    )(proj, proj, proj, proj, proj, cwq, cwk, cwv, alog, dtb, out_norm.reshape(1, -1), tril)


ROUTER_TB = 256


def _router_kernel(x_ref, nw_ref, wr_ref, br_ref, hp_ref, ri_ref, rf_ref, cnt_ref, carry_ref):
    tb = ROUTER_TB

    @pl.when(pl.program_id(0) == 0)
    def _():
        carry_ref[...] = jnp.zeros_like(carry_ref)

    x = x_ref[...]
    ms = jnp.mean(x * x, axis=-1, keepdims=True)
    h = x * lax.rsqrt(ms + RMS_EPS) * nw_ref[...]

    lo = h[:, :HALF_D].astype(BF16).astype(F32)
    hi = h[:, HALF_D:].astype(BF16).astype(F32)
    lo_b = lax.shift_right_logical(lax.bitcast_convert_type(lo, U32), jnp.uint32(16))
    hi_b = lax.bitcast_convert_type(hi, U32) & jnp.uint32(0xFFFF0000)
    hp_ref[...] = hi_b | lo_b

    logits = jnp.dot(h, wr_ref[...], preferred_element_type=F32,
                     precision=lax.Precision.HIGHEST) + br_ref[...]
    lane = lax.broadcasted_iota(I32, (tb, LANES), 1)
    vals, idxs = [], []
    cur = logits
    for _ in range(TOP_K):
        m = jnp.max(cur, axis=-1, keepdims=True)
        idx = jnp.min(jnp.where(cur == m, lane, LANES), axis=-1, keepdims=True)
        vals.append(m)
        idxs.append(idx)
        cur = jnp.where(lane == idx, -jnp.inf, cur)

    exps = [jnp.exp(v - vals[0]) for v in vals]
    denom = exps[0] + exps[1] + exps[2] + exps[3]
    gates = [e / denom for e in exps]

    onehot = jnp.zeros((tb, LANES), F32)
    for idx in idxs:
        onehot = onehot + (lane == idx).astype(F32)
    r_i = lax.broadcasted_iota(I32, (tb, tb), 0)
    c_i = lax.broadcasted_iota(I32, (tb, tb), 1)
    tri = (r_i > c_i).astype(BF16)
    carry = carry_ref[0:1, :]
    cum = _dot(tri, onehot.astype(BF16)) + carry
    carry_new = carry + jnp.sum(onehot, axis=0, keepdims=True)
    carry_ref[...] = jnp.broadcast_to(carry_new, carry_ref.shape)
    cnt_ref[...] = jnp.broadcast_to(carry_new, cnt_ref.shape)

    ri = jnp.zeros((tb, LANES), I32)
    rf = jnp.zeros((tb, LANES), F32)
    for k in range(TOP_K):
        rank = jnp.sum(jnp.where(lane == idxs[k], cum, 0.0), axis=-1, keepdims=True).astype(I32)
        ri = jnp.where(lane == k, idxs[k], ri)
        ri = jnp.where(lane == TOP_K + k, rank, ri)
        rf = jnp.where(lane == k, gates[k], rf)
    ri_ref[...] = ri
    rf_ref[...] = rf


def router(x2, nw, w_router, b_router):
    t, d = x2.shape
    tb = ROUTER_TB
    wr = jnp.zeros((d, LANES), F32).at[:, :N_EXPERTS].set(w_router)
    br = jnp.full((1, LANES), -jnp.inf, F32).at[0, :N_EXPERTS].set(b_router)
    return pl.pallas_call(
        _router_kernel,
        grid=(t // tb,),
        in_specs=[pl.BlockSpec((tb, d), lambda i: (i, 0)),
                  pl.BlockSpec((1, d), lambda i: (0, 0)),
                  pl.BlockSpec((d, LANES), lambda i: (0, 0)),
                  pl.BlockSpec((1, LANES), lambda i: (0, 0))],
        out_specs=[pl.BlockSpec((tb, HALF_D), lambda i: (i, 0)),
                   pl.BlockSpec((tb, LANES), lambda i: (i, 0)),
                   pl.BlockSpec((tb, LANES), lambda i: (i, 0)),
                   pl.BlockSpec((SUBLANES, LANES), lambda i: (0, 0))],
        out_shape=[jax.ShapeDtypeStruct((t, HALF_D), U32),
                   jax.ShapeDtypeStruct((t, LANES), I32),
                   jax.ShapeDtypeStruct((t, LANES), F32),
                   jax.ShapeDtypeStruct((SUBLANES, LANES), F32)],
        scratch_shapes=[pltpu.VMEM((SUBLANES, LANES), F32)],
        compiler_params=_cparams(("arbitrary",)),
        name="moe_router",
    )(x2, nw.reshape(1, d), wr, br)


DISPATCH_TB = 256


def _dispatch_kernel(dest_ref, hp_ref, xs_init_ref, xs_ref, sem):
    del xs_init_ref
    tb = DISPATCH_TB

    def body(r, c):
        for k in range(TOP_K):
            d = dest_ref[0, 0, r * TOP_K + k]
            pltpu.make_async_copy(hp_ref.at[pl.ds(r, 1)], xs_ref.at[pl.ds(d, 1)], sem).start(priority=k % 2)
        return c

    lax.fori_loop(0, tb, body, 0)
    for k in range(TOP_K):
        pltpu.make_async_copy(hp_ref, xs_ref.at[pl.ds(0, tb)], sem).wait()


def dispatch(hp, dest_blocks, n_rows):
    t, w = hp.shape
    tb = DISPATCH_TB
    return pl.pallas_call(
        _dispatch_kernel,
        grid=(t // tb,),
        in_specs=[pl.BlockSpec((1, 1, tb * TOP_K), lambda i: (i, 0, 0), memory_space=pltpu.SMEM),
                  pl.BlockSpec((tb, w), lambda i: (i, 0)),
                  pl.BlockSpec(memory_space=pl.ANY)],
        out_specs=pl.BlockSpec(memory_space=pl.ANY),
        out_shape=jax.ShapeDtypeStruct((n_rows, w), hp.dtype),
        scratch_shapes=[pltpu.SemaphoreType.DMA],
        input_output_aliases={2: 0},
        compiler_params=_cparams(("arbitrary",)),
        name="moe_dispatch",
    )(dest_blocks, hp, jnp.zeros((n_rows, w), hp.dtype))


def _moe_ffn_kernel(te_ref, tb_ref, nr_ref, x_ref, wg_ref, wu_ref, wd_ref, bg_ref, bu_ref, bd_ref, o_ref,
                    xlo_ref, xhi_ref, wgu_ref, wdb_ref, act_ref):
    j = pl.program_id(0)
    c = pl.program_id(1)
    nrows = nr_ref[j]
    full = nrows > TILE_M - SUB_M

    def activations(rows):
        gu = _dot(xlo_ref[rows, :], wgu_ref[0:HALF_D, :])
        gu = gu + _dot(xhi_ref[rows, :], wgu_ref[HALF_D:D_MODEL, :])
        g = jnp.minimum(gu[:, 0:FF_CHUNK] + bg_ref[...], SWIGLU_LIMIT)
        u = jnp.clip(gu[:, FF_CHUNK:2 * FF_CHUNK] + bu_ref[...], -SWIGLU_LIMIT, SWIGLU_LIMIT)
        act_ref[c, rows, :] = ((u + 1.0) * g * jax.nn.sigmoid(SWIGLU_ALPHA * g)).astype(BF16)

    def down(rows):
        acc = _dot(act_ref[0, rows, :], wdb_ref[0:FF_CHUNK, :]) + bd_ref[...]
        for cc in range(1, N_FF_CHUNKS):
            acc = acc + _dot(act_ref[cc, rows, :], wdb_ref[cc * FF_CHUNK:(cc + 1) * FF_CHUNK, :])
        o_ref[rows, :] = acc

    def per_sub_block(fn):
        @pl.when(full)
        def _():
            fn(pl.ds(0, TILE_M))

        for s in range(N_SUB):
            @pl.when(jnp.logical_and(jnp.logical_not(full), s * SUB_M < nrows))
            def _():
                fn(pl.ds(s * SUB_M, SUB_M))

    @pl.when(jnp.logical_and(nrows > 0, c < N_FF_CHUNKS))
    def _():
        @pl.when(c == 0)
        def _():
            for s in range(N_SUB):
                rows = pl.ds(s * SUB_M, SUB_M)
                w = x_ref[rows, :]
                lo = lax.bitcast_convert_type(lax.shift_left(w, jnp.uint32(16)), F32)
                hi = lax.bitcast_convert_type(w & jnp.uint32(0xFFFF0000), F32)
                xlo_ref[rows, :] = lo.astype(BF16)
                xhi_ref[rows, :] = hi.astype(BF16)

        def cast_body(r, carry):
            sl = pl.ds(pl.multiple_of(r * 256, 256), 256)
            wgu_ref[sl, 0:FF_CHUNK] = wg_ref[sl, :].astype(BF16)
            wgu_ref[sl, FF_CHUNK:2 * FF_CHUNK] = wu_ref[sl, :].astype(BF16)
            return carry

        lax.fori_loop(0, D_MODEL // 256, cast_body, 0)
        per_sub_block(activations)

    @pl.when(jnp.logical_and(nrows > 0, c >= N_FF_CHUNKS))
    def _():
        def cast_body(r, carry):
            sl = pl.ds(pl.multiple_of(r * 256, 256), 256)
            wdb_ref[sl, :] = wd_ref[sl, :].astype(BF16)
            return carry

        lax.fori_loop(0, EXPERT_FF // 256, cast_body, 0)
        per_sub_block(down)
        for s in range(N_SUB):
            @pl.when(s * SUB_M >= nrows)
            def _():
                o_ref[pl.ds(s * SUB_M, SUB_M), :] = jnp.zeros((SUB_M, D_CHUNK), F32)

    @pl.when(jnp.logical_and(nrows == 0, c >= N_FF_CHUNKS))
    def _():
        o_ref[...] = jnp.zeros(o_ref.shape, F32)


def moe_ffn(xs, tile_expert, tile_block, tile_rows, layer, w_gate_up, b_gate_up, w_down, b_down):
    n_rows = xs.shape[0]
    depth = w_gate_up.shape[0]
    bgu = b_gate_up.reshape(depth, N_EXPERTS, 1, 2 * EXPERT_FF)
    bd = b_down.reshape(depth, N_EXPERTS, 1, D_MODEL)
    last_ff = N_FF_CHUNKS - 1

    def ff_chunk(c):
        return jnp.minimum(c, last_ff)

    def d_chunk(c):
        return jnp.maximum(c - N_FF_CHUNKS, 0)

    grid_spec = pltpu.PrefetchScalarGridSpec(
        num_scalar_prefetch=3,
        grid=(n_rows // TILE_M, N_FF_CHUNKS + N_D_CHUNKS),
        in_specs=[
            pl.BlockSpec((TILE_M, HALF_D), lambda j, c, te, tb, nr: (tb[j], 0)),
            pl.BlockSpec((None, None, D_MODEL, FF_CHUNK), lambda j, c, te, tb, nr: (layer, te[j], 0, ff_chunk(c))),
            pl.BlockSpec((None, None, D_MODEL, FF_CHUNK),
                         lambda j, c, te, tb, nr: (layer, te[j], 0, N_FF_CHUNKS + ff_chunk(c))),
            pl.BlockSpec((None, None, EXPERT_FF, D_CHUNK), lambda j, c, te, tb, nr: (layer, te[j], 0, d_chunk(c))),
            pl.BlockSpec((None, None, 1, FF_CHUNK), lambda j, c, te, tb, nr: (layer, te[j], 0, ff_chunk(c))),
            pl.BlockSpec((None, None, 1, FF_CHUNK),
                         lambda j, c, te, tb, nr: (layer, te[j], 0, N_FF_CHUNKS + ff_chunk(c))),
            pl.BlockSpec((None, None, 1, D_CHUNK), lambda j, c, te, tb, nr: (layer, te[j], 0, d_chunk(c))),
        ],
        out_specs=pl.BlockSpec((TILE_M, D_CHUNK), lambda j, c, te, tb, nr: (j, d_chunk(c))),
        scratch_shapes=[pltpu.VMEM((TILE_M, HALF_D), BF16),
                        pltpu.VMEM((TILE_M, HALF_D), BF16),
                        pltpu.VMEM((D_MODEL, 2 * FF_CHUNK), BF16),
                        pltpu.VMEM((EXPERT_FF, D_CHUNK), BF16),
                        pltpu.VMEM((N_FF_CHUNKS, TILE_M, FF_CHUNK), BF16)],
    )
    return pl.pallas_call(
        _moe_ffn_kernel,
        grid_spec=grid_spec,
        out_shape=jax.ShapeDtypeStruct((n_rows, D_MODEL), F32),
        compiler_params=_cparams(("arbitrary", "arbitrary")),
        name="moe_ffn",
    )(tile_expert, tile_block, tile_rows, xs, w_gate_up, w_gate_up, w_down, bgu, bgu, bd)


COMBINE_TB = 256


def _combine_kernel(dest_ref, x_ref, g_ref, ys_ref, o_ref, buf_ref, sem):
    tb = COMBINE_TB

    def body(r, c):
        for k in range(TOP_K):
            d = dest_ref[0, 0, r * TOP_K + k]
            pltpu.make_async_copy(ys_ref.at[pl.ds(d, 1)], buf_ref.at[k, pl.ds(r, 1)], sem).start(priority=k % 2)
        return c

    lax.fori_loop(0, tb, body, 0)
    for k in range(TOP_K):
        pltpu.make_async_copy(ys_ref.at[pl.ds(0, tb)], buf_ref.at[k], sem).wait()
    acc = x_ref[...]
    g = g_ref[...]
    for k in range(TOP_K):
        acc = acc + g[:, k:k + 1] * buf_ref[k]
    o_ref[...] = acc


def combine(x2, gates, ys, dest_blocks):
    t, d = x2.shape
    tb = COMBINE_TB
    return pl.pallas_call(
        _combine_kernel,
        grid=(t // tb,),
        in_specs=[pl.BlockSpec((1, 1, tb * TOP_K), lambda i: (i, 0, 0), memory_space=pltpu.SMEM),
                  pl.BlockSpec((tb, d), lambda i: (i, 0)),
                  pl.BlockSpec((tb, LANES), lambda i: (i, 0)),
                  pl.BlockSpec(memory_space=pl.ANY)],
        out_specs=pl.BlockSpec((tb, d), lambda i: (i, 0)),
        out_shape=jax.ShapeDtypeStruct((t, d), F32),
        scratch_shapes=[pltpu.VMEM((TOP_K, tb, d), F32), pltpu.SemaphoreType.DMA],
        compiler_params=_cparams(("arbitrary",)),
        name="moe_combine",
    )(dest_blocks, x2, gates, ys)


def moe_layer(x2, nw, w_router, b_router, layer, w_gate_up, b_gate_up, w_down, b_down):
    t = x2.shape[0]
    max_tiles = (t * TOP_K) // TILE_M + N_EXPERTS
    hp, ri, rf, cnt = router(x2, nw, w_router, b_router)
    idx = ri[:, 0:TOP_K]
    rank = ri[:, TOP_K:2 * TOP_K]
    counts = cnt[0, :N_EXPERTS].astype(I32)

    tiles_e = (counts + TILE_M - 1) // TILE_M
    tile_end = jnp.cumsum(tiles_e)
    tile_start = tile_end - tiles_e
    n_tiles = tile_end[-1]
    dest = (tile_start * TILE_M)[idx] + rank
    j = jnp.arange(max_tiles, dtype=I32)
    jc = jnp.minimum(j, n_tiles - 1)
    te = jnp.sum((jc[:, None] >= tile_end[None, :]).astype(I32), axis=1)
    rows_left = counts[te] - (jc - tile_start[te]) * TILE_M
    tile_rows = jnp.where(j < n_tiles, jnp.clip(rows_left, 0, TILE_M), 0).astype(I32)
    dest_blocks = dest.reshape(t // DISPATCH_TB, 1, DISPATCH_TB * TOP_K).astype(I32)

    xs = dispatch(hp, dest_blocks, max_tiles * TILE_M)
    ys = moe_ffn(xs, te.astype(I32), jc.astype(I32), tile_rows, layer, w_gate_up, b_gate_up, w_down, b_down)
    return combine(x2, rf, ys, dest_blocks)


def even_mixer(x2, bsz, seq, nw, w_in, conv_w, conv_b, dt_bias, a_log, d_skip, ssd_norm, q_norm, k_norm, w_out):
    d = x2.shape[1]
    o2 = SSD_INNER + SSD_INNER + SSD_BC
    o3 = o2 + SSD_HEADS
    w_cat = jnp.concatenate(
        [w_in[:, :o2], w_in[:, o3:], w_in[:, o2:o3], jnp.zeros((d, PROJ_TN - SSD_HEADS), F32)], axis=1).astype(BF16)
    proj = norm_matmul(x2, nw, w_cat)
    y_ssd = ssd_mixer(proj, conv_w, conv_b, dt_bias, a_log, d_skip, ssd_norm, bsz, seq,
                      z_col=0, x_col=1, bc_col=(2 * SSD_INNER) // SSD_BC, dt_col=(o2 + 3 * SB_INNER) // LANES)
    y_sb = sb_attention(proj, q_norm, k_norm, bsz, seq, q_col=o2 // LANES, k_col=(o2 + SB_INNER) // LANES,
                        v_col=(o2 + 2 * SB_INNER) // LANES)
    w_out_bf = w_out.astype(BF16)
    return matmul_residual([y_ssd, y_sb], [w_out_bf[:SSD_INNER], w_out_bf[SSD_INNER:]], x2)


def odd_mixer(x2, bsz, seq, nw, w_in, conv_w, dt_bias, a_log, out_norm, w_out):
    d = x2.shape[1]
    n_in = w_in.shape[1]
    n_pad = -n_in % PROJ_TN
    w_cat = jnp.concatenate([w_in, jnp.zeros((d, n_pad), F32)], axis=1).astype(BF16)
    proj = norm_matmul(x2, nw, w_cat)
    qw = GDN_KH_STEP * GDN_DIM
    o = gdn_mixer(proj, conv_w, dt_bias, a_log, out_norm, bsz, seq,
                  q_col=0, k_col=GDN_KEY // qw, v_col=(2 * GDN_KEY) // (2 * qw), z_col=(2 * GDN_KEY + GDN_VAL) // (2 * qw))
    return matmul_residual([o], [w_out.astype(BF16)], x2)


def kernel(x, mix_norm, ffn_norm, ev_w_in, ev_conv_w, ev_conv_b, ev_dt_bias, ev_a_log, ev_d_skip, ev_ssd_norm, ev_q_norm, ev_k_norm, ev_w_out, od_w_in, od_conv_w, od_dt_bias, od_a_log, od_out_norm, od_w_out, moe_w_router, moe_b_router, moe_w_gate_up, moe_b_gate_up, moe_w_down, moe_b_down):
    bsz, seq, d = x.shape
    depth = mix_norm.shape[0]
    x2 = x.reshape(bsz * seq, d)
    for layer in range(depth):
        i = layer // 2
        if layer % 2 == 0:
            x2 = even_mixer(x2, bsz, seq, mix_norm[layer], ev_w_in[i], ev_conv_w[i], ev_conv_b[i], ev_dt_bias[i],
                            ev_a_log[i], ev_d_skip[i], ev_ssd_norm[i], ev_q_norm[i], ev_k_norm[i], ev_w_out[i])
        else:
            x2 = odd_mixer(x2, bsz, seq, mix_norm[layer], od_w_in[i], od_conv_w[i], od_dt_bias[i], od_a_log[i],
                           od_out_norm[i], od_w_out[i])
        x2 = moe_layer(x2, ffn_norm[layer], moe_w_router[layer], moe_b_router[layer], layer,
                       moe_w_gate_up, moe_b_gate_up, moe_w_down, moe_b_down)
    return x2.reshape(bsz, seq, d)
```
